```python
import math
import jax, jax.numpy as jnp
from jax import lax
import numpy as np

D_MODEL = 1024
BATCH = 2
SEQ = 8192
DEPTH = 2
DEC_BATCH = 32
DEC_SEQ = 32
PAST_LEN = 4096

CHUNK = 64
QBLK = 128
PLE_DIM = 256
ROPE_THETA = 500000.0
EPS = 1e-6
NEG_INF = -1e30
A_HEADS = 8
A_DH = 64
A_WIDTH = A_HEADS * 2 * A_DH
A_ROT = A_DH // 4
B_HEADS = 16
B_DH = 64
B_WIDTH = B_HEADS * B_DH
N_A = (DEPTH + 1) // 2
N_B = DEPTH // 2

kernel_name = 'hybrid_diffattn_stickbreak_stream_step'


def _rmsnorm(x, g):
    xf = x.astype(jnp.float32)
    y = xf * lax.rsqrt(jnp.mean(xf * xf, axis=-1, keepdims=True) + EPS)
    return (y * g.astype(jnp.float32)).astype(x.dtype)


def _rope(x, pos):
    half = A_ROT // 2
    inv = ROPE_THETA ** (-jnp.arange(half, dtype=jnp.float32) * 2.0 / A_ROT)
    ang = pos.astype(jnp.float32)[:, None] * inv[None, :]
    shape = (1, x.shape[1]) + (1,) * (x.ndim - 3) + (half,)
    c = jnp.cos(ang).reshape(shape)
    s = jnp.sin(ang).reshape(shape)
    xf = x.astype(jnp.float32)
    x1 = xf[..., :half]
    x2 = xf[..., half:A_ROT]
    out = jnp.concatenate([x1 * c - x2 * s, x2 * c + x1 * s, xf[..., A_ROT:]], axis=-1)
    return out.astype(x.dtype)


def _sweep(block_fn, q, q_pos):
    b, t = q.shape[:2]
    if t <= QBLK:
        return block_fn(q, q_pos)
    nb = t // QBLK
    qb = jnp.moveaxis(q.reshape((b, nb, QBLK) + q.shape[2:]), 1, 0)
    pb = q_pos.reshape(nb, QBLK)
    ob = lax.map(lambda a: block_fn(a[0], a[1]), (qb, pb))
    return jnp.moveaxis(ob, 0, 1).reshape((b, t) + ob.shape[3:])


def _with_past(new, past, pos):
    if past is None:
        return new, pos
    k_pos = jnp.concatenate([jnp.arange(past.shape[1], dtype=jnp.int32), pos])
    return jnp.concatenate([past.astype(new.dtype), new], axis=1), k_pos


def _diff_attn(h, pos, w_in, lq1, lk1, lq2, lk2, sub_g, w_out, lam_init, k_past, v_past):
    b, t, _ = h.shape
    q, k, v, g = jnp.split(h @ w_in, 4, axis=-1)
    q = _rope(q.reshape(b, t, A_HEADS, 2, A_DH), pos)
    k = _rope(k.reshape(b, t, A_HEADS, 2, A_DH), pos)
    v = v.reshape(b, t, A_HEADS, 2 * A_DH)
    k_all, k_pos = _with_past(k, k_past, pos)
    v_all, _ = _with_past(v, v_past, pos)
    f32 = jnp.float32
    lam = (jnp.exp(jnp.sum(lq1.astype(f32) * lk1.astype(f32)))
           - jnp.exp(jnp.sum(lq2.astype(f32) * lk2.astype(f32))) + lam_init)
    kf = k_all.astype(f32)
    vf = v_all.astype(f32)
    k_chunk = k_pos // CHUNK

    def block(qb, qpb):
        s = jnp.einsum('bqhcd,bkhcd->bhcqk', qb.astype(f32), kf) * (A_DH ** -0.5)
        mask = k_chunk[None, :] <= (qpb // CHUNK)[:, None]
        pr = jax.nn.softmax(jnp.where(mask, s, NEG_INF), axis=-1)
        w = pr[:, :, 0] - lam * pr[:, :, 1]
        return jnp.einsum('bhqk,bkhe->bqhe', w, vf)

    o = _sweep(block, q, pos)
    o = _rmsnorm(o, sub_g) * (1.0 - lam_init)
    o = o.reshape(b, t, A_WIDTH).astype(h.dtype) * jax.nn.silu(g)
    return o @ w_out, k, v


def _stick_breaking(h, pos, w_in, w_out, k_past, v_past):
    b, t, _ = h.shape
    q, k, v, g = jnp.split(h @ w_in, 4, axis=-1)
    q = q.reshape(b, t, B_HEADS, B_DH)
    k = k.reshape(b, t, B_HEADS, B_DH)
    v = v.reshape(b, t, B_HEADS, B_DH)
    k_all, k_pos = _with_past(k, k_past, pos)
    v_all, _ = _with_past(v, v_past, pos)
    f32 = jnp.float32
    kf = k_all.astype(f32)
    vf = v_all.astype(f32)

    def block(qb, qpb):
        z = jnp.einsum('bqhd,bkhd->bhqk', qb.astype(f32), kf) * (B_DH ** -0.5)
        valid = k_pos[None, :] < qpb[:, None]
        log_keep = jnp.where(valid, jax.nn.log_sigmoid(-z), 0.0)
        after = lax.cumsum(log_keep, axis=3, reverse=True) - log_keep
        a = jnp.where(valid, jnp.exp(jax.nn.log_sigmoid(z) + after), 0.0)
        return jnp.einsum('bhqk,bkhd->bqhd', a, vf)

    o = _sweep(block, q, pos)
    o = o.reshape(b, t, B_WIDTH).astype(h.dtype) * jax.nn.silu(g)
    return o @ w_out, k, v


def _trunk(x, p, pos, a_norm_g, a_w_in, a_lam_q1, a_lam_k1, a_lam_q2, a_lam_k2, a_subln_g, a_w_out,
           b_norm_g, b_w_in, b_w_out, ple_w_proj, ple_w_gate, final_norm_g,
           a_k_past, a_v_past, b_k_past, b_v_past):
    ak, av, bk, bv = [], [], [], []
    for i in range(DEPTH):
        j = i // 2
        if i % 2 == 0:
            lam_init = 0.8 - 0.6 * math.exp(-0.3 * i)
            y, k, v = _diff_attn(_rmsnorm(x, a_norm_g[j]), pos, a_w_in[j], a_lam_q1[j], a_lam_k1[j],
                                 a_lam_q2[j], a_lam_k2[j], a_subln_g[j], a_w_out[j], lam_init,
                                 None if a_k_past is None else a_k_past[j],
                                 None if a_v_past is None else a_v_past[j])
            ak.append(k)
            av.append(v)
        else:
            y, k, v = _stick_breaking(_rmsnorm(x, b_norm_g[j]), pos, b_w_in[j], b_w_out[j],
                                      None if b_k_past is None else b_k_past[j],
                                      None if b_v_past is None else b_v_past[j])
            bk.append(k)
            bv.append(v)
        x = x + y
        x = x + jax.nn.sigmoid(x @ ple_w_gate[i]) * (p[i] @ ple_w_proj[i])
    return _rmsnorm(x, final_norm_g), jnp.stack(ak), jnp.stack(av), jnp.stack(bk), jnp.stack(bv)


def setup_inputs(seed: int = 0) -> dict:
    key = jax.random.key(seed)
    ks = jax.random.split(key, 24)
    f32 = jnp.float32

    def nrm(k, shape, scale=1.0):
        return jax.random.normal(k, shape, f32) * scale

    return {
        'x_prompt': nrm(ks[0], (BATCH, SEQ, D_MODEL)),
        'x_sample': nrm(ks[1], (DEC_BATCH, DEC_SEQ, D_MODEL)),
        'cache_a_k': nrm(ks[2], (N_A, DEC_BATCH, PAST_LEN, A_HEADS, 2, A_DH)),
        'cache_a_v': nrm(ks[3], (N_A, DEC_BATCH, PAST_LEN, A_HEADS, 2 * A_DH)),
        'cache_b_k': nrm(ks[4], (N_B, DEC_BATCH, PAST_LEN, B_HEADS, B_DH)),
        'cache_b_v': nrm(ks[5], (N_B, DEC_BATCH, PAST_LEN, B_HEADS, B_DH)),
        'p_prompt': nrm(ks[6], (DEPTH, BATCH, SEQ, PLE_DIM)),
        'p_sample': nrm(ks[7], (DEPTH, DEC_BATCH, DEC_SEQ, PLE_DIM)),
        'a_norm_g': 1.0 + nrm(ks[8], (N_A, D_MODEL), 0.02),
        'a_w_in': nrm(ks[9], (N_A, D_MODEL, 4 * A_WIDTH), D_MODEL ** -0.5),
        'a_lam_q1': nrm(ks[10], (N_A, A_DH), 0.1),
        'a_lam_k1': nrm(ks[11], (N_A, A_DH), 0.1),
        'a_lam_q2': nrm(ks[12], (N_A, A_DH), 0.1),
        'a_lam_k2': nrm(ks[13], (N_A, A_DH), 0.1),
        'a_subln_g': 1.0 + nrm(ks[14], (N_A, 2 * A_DH), 0.02),
        'a_w_out': nrm(ks[15], (N_A, A_WIDTH, D_MODEL), A_WIDTH ** -0.5),
        'b_norm_g': 1.0 + nrm(ks[16], (N_B, D_MODEL), 0.02),
        'b_w_in': nrm(ks[17], (N_B, D_MODEL, 4 * B_WIDTH), D_MODEL ** -0.5),
        'b_w_out': nrm(ks[18], (N_B, B_WIDTH, D_MODEL), B_WIDTH ** -0.5),
        'ple_w_proj': nrm(ks[19], (DEPTH, PLE_DIM, D_MODEL), PLE_DIM ** -0.5),
        'ple_w_gate': nrm(ks[20], (DEPTH, D_MODEL, D_MODEL), D_MODEL ** -0.5),
        'final_norm_g': 1.0 + nrm(ks[21], (D_MODEL,), 0.02),
    }


def reference(x_prompt, x_sample, cache_a_k, cache_a_v, cache_b_k, cache_b_v, p_prompt, p_sample,
              a_norm_g, a_w_in, a_lam_q1, a_lam_k1, a_lam_q2, a_lam_k2, a_subln_g, a_w_out,
              b_norm_g, b_w_in, b_w_out, ple_w_proj, ple_w_gate, final_norm_g):
    pos_p = jnp.arange(x_prompt.shape[1], dtype=jnp.int32)
    pos_s = cache_a_k.shape[2] + jnp.arange(x_sample.shape[1], dtype=jnp.int32)
    y_prompt, ak_p, av_p, bk_p, bv_p = _trunk(
        x_prompt, p_prompt, pos_p, a_norm_g, a_w_in, a_lam_q1, a_lam_k1, a_lam_q2, a_lam_k2,
        a_subln_g, a_w_out, b_norm_g, b_w_in, b_w_out, ple_w_proj, ple_w_gate, final_norm_g,
        None, None, None, None)
    y_sample, ak_s, av_s, bk_s, bv_s = _trunk(
        x_sample, p_sample, pos_s, a_norm_g, a_w_in, a_lam_q1, a_lam_k1, a_lam_q2, a_lam_k2,
        a_subln_g, a_w_out, b_norm_g, b_w_in, b_w_out, ple_w_proj, ple_w_gate, final_norm_g,
        cache_a_k, cache_a_v, cache_b_k, cache_b_v)
    return (y_prompt, y_sample, ak_p, av_p, bk_p, bv_p, ak_s, av_s, bk_s, bv_s)
```

```python
import functools
import math

import jax
import jax.numpy as jnp
from jax import lax
from jax.experimental import pallas as pl
from jax.experimental.pallas import tpu as pltpu

F32 = jnp.float32
BF16 = jnp.bfloat16

D_MODEL = 1024
CHUNK = 64
PLE_DIM = 256
ROPE_THETA = 500000.0
EPS = 1e-6
NEG_INF = -1e30
A_HEADS = 8
A_DH = 64
A_ROT = A_DH // 4
B_HEADS = 16
B_DH = 64
WIDTH = 1024
LANES = 128
HALF = 64
Q_SCALE = A_DH ** -0.5

PRE_ROWS = 256
ATTN_BLK = 256
VMEM_LIMIT = 48 * 1024 * 1024

_NT = (((1,), (1,)), ((), ()))


def _params(n_axes):
    return pltpu.CompilerParams(
        dimension_semantics=("arbitrary",) * n_axes, vmem_limit_bytes=VMEM_LIMIT)


def _sigmoid(x):
    return 1.0 / (1.0 + jnp.exp(-x))


def _pre_body(*refs, rope):
    if rope:
        x_ref, g_ref, w_ref, c_ref, s1_ref, s2_ref = refs[:6]
        outs = refs[6:]
    else:
        x_ref, g_ref, w_ref = refs[:3]
        outs = refs[3:]
    q_ref, k_ref, kb_ref, v_ref, vb_ref, gate_ref = outs

    x = x_ref[...]
    ms = jnp.mean(x * x, axis=-1, keepdims=True)
    hb = ((x * lax.rsqrt(ms + EPS)) * g_ref[...]).astype(BF16)

    def proj(part):
        return jnp.dot(hb, w_ref[:, part * WIDTH:(part + 1) * WIDTH],
                       preferred_element_type=F32)

    def rot(r, j):
        sl = r[:, j * LANES:(j + 1) * LANES]
        if not rope:
            return sl
        return (sl * c_ref[...] + pltpu.roll(sl, LANES - A_ROT // 2, 1) * s1_ref[...]
                + pltpu.roll(sl, A_ROT // 2, 1) * s2_ref[...])

    q = proj(0)
    for j in range(WIDTH // LANES):
        cols = slice(j * LANES, (j + 1) * LANES)
        q_ref[:, cols] = (rot(q, j) * Q_SCALE).astype(BF16)
    k = proj(1)
    for j in range(WIDTH // LANES):
        cols = slice(j * LANES, (j + 1) * LANES)
        kr = rot(k, j)
        k_ref[:, cols] = kr
        kb_ref[:, cols] = kr.astype(BF16)
    v = proj(2)
    v_ref[...] = v
    vb_ref[...] = v.astype(BF16)
    gate_ref[...] = proj(3)


def _pre(x, norm_g, w_in, tables):
    n = x.shape[0]
    rows = PRE_ROWS
    rope = tables is not None
    row_spec = lambda width: pl.BlockSpec((rows, width), lambda i: (i, 0))
    in_specs = [row_spec(D_MODEL),
                pl.BlockSpec((1, D_MODEL), lambda i: (0, 0)),
                pl.BlockSpec((D_MODEL, 4 * WIDTH), lambda i: (0, 0))]
    args = [x, norm_g.reshape(1, D_MODEL), w_in]
    if rope:
        n_tab = tables[0].shape[0] // rows
        tab_spec = pl.BlockSpec((rows, LANES), lambda i: (i % n_tab, 0))
        in_specs += [tab_spec] * 3
        args += list(tables)
    out_shape = [jax.ShapeDtypeStruct((n, WIDTH), dt) for dt in (BF16, F32, BF16, F32, BF16, F32)]
    return pl.pallas_call(
        functools.partial(_pre_body, rope=rope),
        grid=(n // rows,),
        in_specs=in_specs,
        out_specs=[row_spec(WIDTH)] * 6,
        out_shape=out_shape,
        compiler_params=_params(1),
        name="pre_rope" if rope else "pre",
    )(*args)


def _rope_tables(pos, rows):
    half = A_ROT // 2
    inv = ROPE_THETA ** (-jnp.arange(half, dtype=F32) * 2.0 / A_ROT)
    ang = pos.astype(F32)[:, None] * inv[None, :]
    cos, sin = jnp.cos(ang), jnp.sin(ang)
    t = pos.shape[0]
    ones = jnp.ones((t, HALF - A_ROT), F32)
    zeros = jnp.zeros((t, HALF - half), F32)
    c = jnp.concatenate([cos, cos, ones], axis=1)
    s1 = jnp.concatenate([-sin, zeros], axis=1)
    s2 = jnp.concatenate([jnp.zeros((t, half), F32), sin, jnp.zeros((t, HALF - A_ROT), F32)], axis=1)
    reps = (max(rows // t, 1), LANES // HALF)
    return tuple(jnp.tile(a, reps) for a in (c, s1, s2))


def _mod_pow2(x, n):
    assert n & (n - 1) == 0
    return x & (n - 1)


def _chunk_of(pos):
    assert CHUNK & (CHUNK - 1) == 0
    return pos >> (CHUNK.bit_length() - 1)


def _lo_lanes():
    return lax.broadcasted_iota(jnp.int32, (1, LANES), 1) < HALF


def _split_halves(q):
    lo = _lo_lanes()
    zero = jnp.zeros_like(q)
    return jnp.concatenate([jnp.where(lo, q, zero), jnp.where(lo, zero, q)], axis=0)


def _lam(lam_ref, lam_init):
    a = jnp.sum(lam_ref[0:1, :] * lam_ref[1:2, :], axis=-1, keepdims=True)
    b = jnp.sum(lam_ref[2:3, :] * lam_ref[3:4, :], axis=-1, keepdims=True)
    return jnp.exp(a) - jnp.exp(b) + lam_init


def _diff_epilogue(o1, o2, lam_ref, subg_ref, g_ref, o_ref, lam_init):
    o = o1 - _lam(lam_ref, lam_init) * o2
    ms = jnp.mean(o * o, axis=-1, keepdims=True)
    o = (o * lax.rsqrt(ms + EPS)) * subg_ref[...] * (1.0 - lam_init)
    g = g_ref[...]
    o_ref[...] = (o * (g * _sigmoid(g))).astype(o_ref.dtype)


def _stick_terms(z, valid):
    lk = jnp.minimum(-z, 0.0) - jnp.log(1.0 + jnp.exp(-jnp.abs(z)))
    if valid is not None:
        lk = jnp.where(valid, lk, 0.0)
    hi = lk.astype(BF16)
    lo = (lk - hi.astype(F32)).astype(BF16)
    return lk, hi, lo


def _suffix_sums(hi, lo, tri):
    return (jnp.dot(hi, tri, preferred_element_type=F32)
            + jnp.dot(lo, tri, preferred_element_type=F32))


def _attn_a_prompt_body(lam_ref, subg_ref, q_ref, k_ref, v_ref, g_ref, o_ref,
                        qq_ref, m_ref, l_ref, acc_ref, *, blk, lam_init):
    qi = pl.program_id(2)
    qq_ref[...] = _split_halves(q_ref[...])
    m_ref[...] = jnp.full(m_ref.shape, NEG_INF, F32)
    l_ref[...] = jnp.zeros(l_ref.shape, F32)
    acc_ref[...] = jnp.zeros(acc_ref.shape, F32)

    def step(kj, masked):
        start = pl.multiple_of(kj * blk, blk)
        k = k_ref[pl.ds(start, blk), :]
        v = v_ref[pl.ds(start, blk), :]
        s = lax.dot_general(qq_ref[...], k, _NT, preferred_element_type=F32)
        if masked:
            row = lax.broadcasted_iota(jnp.int32, s.shape, 0)
            col = lax.broadcasted_iota(jnp.int32, s.shape, 1)
            vis = _chunk_of(col) <= _chunk_of(_mod_pow2(row, blk))
            s = jnp.where(vis, s, NEG_INF)
        m_prev = m_ref[...]
        m_new = jnp.maximum(m_prev, jnp.max(s, axis=-1, keepdims=True))
        alpha = jnp.exp(m_prev - m_new)
        p = jnp.exp(s - jnp.tile(m_new, (1, blk // LANES)))
        l_ref[...] = alpha * l_ref[...] + jnp.sum(p, axis=-1, keepdims=True)
        acc_ref[...] = alpha * acc_ref[...] + jnp.dot(p.astype(BF16), v, preferred_element_type=F32)
        m_ref[...] = m_new

    def body(kj, c):
        step(kj, False)
        return c

    lax.fori_loop(0, qi, body, 0)
    step(qi, True)

    acc = acc_ref[...]
    l = l_ref[...]
    _diff_epilogue(acc[:blk] / l[:blk], acc[blk:] / l[blk:], lam_ref, subg_ref, g_ref, o_ref, lam_init)


def _attn_a_prompt(q, kb, vb, gate, lam4, subg, lam_init):
    b, t, _ = q.shape
    blk = ATTN_BLK
    q_spec = pl.BlockSpec((None, blk, LANES), lambda bi, h, qi: (bi, qi, h))
    kv_spec = pl.BlockSpec((None, t, LANES), lambda bi, h, qi: (bi, 0, h))
    full = lambda shape: pl.BlockSpec(shape, lambda bi, h, qi: (0,) * len(shape))
    return pl.pallas_call(
        functools.partial(_attn_a_prompt_body, blk=blk, lam_init=lam_init),
        grid=(b, A_HEADS, t // blk),
        in_specs=[full(lam4.shape), full(subg.shape), q_spec, kv_spec, kv_spec, q_spec],
        out_specs=q_spec,
        out_shape=jax.ShapeDtypeStruct((b, t, WIDTH), BF16),
        scratch_shapes=[pltpu.VMEM((2 * blk, LANES), BF16),
                        pltpu.VMEM((2 * blk, LANES), F32),
                        pltpu.VMEM((2 * blk, LANES), F32),
                        pltpu.VMEM((2 * blk, LANES), F32)],
        compiler_params=_params(3),
        name="attn_a_prompt",
    )(lam4, subg, q, kb, vb, gate)


def _attn_b_prompt_body(tri_ref, q_ref, k_ref, v_ref, g_ref, o_ref,
                        qq_ref, carry_ref, acc_ref, *, blk):
    qi = pl.program_id(2)
    qq_ref[...] = _split_halves(q_ref[...])
    carry_ref[...] = jnp.zeros(carry_ref.shape, F32)
    acc_ref[...] = jnp.zeros(acc_ref.shape, F32)

    def step(kj, masked):
        start = pl.multiple_of(kj * blk, blk)
        k = k_ref[pl.ds(start, blk), :]
        v = v_ref[pl.ds(start, blk), :]
        z = lax.dot_general(qq_ref[...], k, _NT, preferred_element_type=F32)
        valid = None
        if masked:
            row = lax.broadcasted_iota(jnp.int32, z.shape, 0)
            col = lax.broadcasted_iota(jnp.int32, z.shape, 1)
            valid = col < _mod_pow2(row, blk)
        lk, hi, lo = _stick_terms(z, valid)
        carry = carry_ref[...]
        a = jnp.exp(z + _suffix_sums(hi, lo, tri_ref[...]) + jnp.tile(carry, (1, blk // LANES)))
        if masked:
            a = jnp.where(valid, a, 0.0)
        acc_ref[...] += jnp.dot(a.astype(BF16), v, preferred_element_type=F32)
        carry_ref[...] = carry + jnp.sum(lk, axis=-1, keepdims=True)

    step(qi, True)

    def body(t, c):
        step(qi - 1 - t, False)
        return c

    lax.fori_loop(0, qi, body, 0)

    acc = acc_ref[...]
    g = g_ref[...]
    o = jnp.where(_lo_lanes(), acc[:blk], acc[blk:])
    o_ref[...] = (o * (g * _sigmoid(g))).astype(o_ref.dtype)


def _attn_b_prompt(q, kb, vb, gate, tri):
    b, t, _ = q.shape
    blk = ATTN_BLK
    q_spec = pl.BlockSpec((None, blk, LANES), lambda bi, h, qi: (bi, qi, h))
    kv_spec = pl.BlockSpec((None, t, LANES), lambda bi, h, qi: (bi, 0, h))
    return pl.pallas_call(
        functools.partial(_attn_b_prompt_body, blk=blk),
        grid=(b, WIDTH // LANES, t // blk),
        in_specs=[pl.BlockSpec((blk, blk), lambda bi, h, qi: (0, 0)),
                  q_spec, kv_spec, kv_spec, q_spec],
        out_specs=q_spec,
        out_shape=jax.ShapeDtypeStruct((b, t, WIDTH), BF16),
        scratch_shapes=[pltpu.VMEM((2 * blk, LANES), BF16),
                        pltpu.VMEM((2 * blk, LANES), F32),
                        pltpu.VMEM((2 * blk, LANES), F32)],
        compiler_params=_params(3),
        name="attn_b_prompt",
    )(tri, q, kb, vb, gate)


def _pad_rows(x, rows):
    return jnp.concatenate([x, jnp.zeros((rows - x.shape[0], x.shape[1]), x.dtype)], axis=0)


def _attn_a_sample_body(lam_ref, subg_ref, q_ref, kn_ref, vn_ref, kp_ref, vp_ref, g_ref, o_ref,
                        *, t_new, past, lam_init):
    qq = _split_halves(q_ref[...])
    kp = kp_ref[...].astype(BF16)
    vp = vp_ref[...].astype(BF16)
    s_p = lax.dot_general(qq, kp, _NT, preferred_element_type=F32)
    kn = _pad_rows(kn_ref[...], LANES)
    vn = _pad_rows(vn_ref[...], LANES)
    s_n = lax.dot_general(qq, kn, _NT, preferred_element_type=F32)
    row = lax.broadcasted_iota(jnp.int32, s_n.shape, 0)
    col = lax.broadcasted_iota(jnp.int32, s_n.shape, 1)
    vis = jnp.logical_and(col < t_new,
                          _chunk_of(past + col) <= _chunk_of(past + _mod_pow2(row, t_new)))
    s_n = jnp.where(vis, s_n, NEG_INF)
    m = jnp.maximum(jnp.max(s_p, axis=-1, keepdims=True), jnp.max(s_n, axis=-1, keepdims=True))
    p_p = jnp.exp(s_p - m)
    p_n = jnp.exp(s_n - m)
    l = jnp.sum(p_p, axis=-1, keepdims=True) + jnp.sum(p_n, axis=-1, keepdims=True)
    acc = (jnp.dot(p_p.astype(BF16), vp, preferred_element_type=F32)
           + jnp.dot(p_n.astype(BF16), vn, preferred_element_type=F32))
    o = acc / l
    _diff_epilogue(o[:t_new], o[t_new:], lam_ref, subg_ref, g_ref, o_ref, lam_init)


def _attn_a_sample(q, kb, vb, k_past, v_past, gate, lam4, subg, lam_init):
    b, t_new, _ = q.shape
    past = k_past.shape[1]
    new_spec = pl.BlockSpec((None, t_new, LANES), lambda bi, h: (bi, 0, h))
    past_spec = pl.BlockSpec((None, past, LANES), lambda bi, h: (bi, 0, h))
    full = lambda shape: pl.BlockSpec(shape, lambda bi, h: (0,) * len(shape))
    return pl.pallas_call(
        functools.partial(_attn_a_sample_body, t_new=t_new, past=past, lam_init=lam_init),
        grid=(b, A_HEADS),
        in_specs=[full(lam4.shape), full(subg.shape), new_spec, new_spec, new_spec,
                  past_spec, past_spec, new_spec],
        out_specs=new_spec,
        out_shape=jax.ShapeDtypeStruct((b, t_new, WIDTH), BF16),
        compiler_params=_params(2),
        name="attn_a_sample",
    )(lam4, subg, q, kb, vb, k_past, v_past, gate)


def _attn_b_sample_body(tri_ref, q_ref, kn_ref, vn_ref, kp_ref, vp_ref, g_ref, o_ref,
                        *, t_new, past, blk):
    rows = 2 * t_new
    n_blk = past // blk
    qq = _split_halves(q_ref[...])
    tri = tri_ref[...]

    kn = _pad_rows(kn_ref[...], blk)
    vn = _pad_rows(vn_ref[...], blk)
    z_n = lax.dot_general(qq, kn, _NT, preferred_element_type=F32)
    row = lax.broadcasted_iota(jnp.int32, z_n.shape, 0)
    col = lax.broadcasted_iota(jnp.int32, z_n.shape, 1)
    valid = col < _mod_pow2(row, t_new)
    lk_n, hi_n, lo_n = _stick_terms(z_n, valid)
    a_n = jnp.where(valid, jnp.exp(z_n + _suffix_sums(hi_n, lo_n, tri)), 0.0)
    acc = jnp.dot(a_n.astype(BF16), vn, preferred_element_type=F32)
    running = jnp.sum(lk_n, axis=-1, keepdims=True)

    kp = kp_ref[...].astype(BF16)
    vp = vp_ref[...].astype(BF16)
    z_all = lax.dot_general(qq, kp, _NT, preferred_element_type=F32)
    z = jnp.concatenate([z_all[:, c * blk:(c + 1) * blk] for c in range(n_blk)], axis=0)
    lk, hi, lo = _stick_terms(z, None)
    suffix = _suffix_sums(hi, lo, tri)
    tot = jnp.sum(lk, axis=-1, keepdims=True)
    carries = [None] * n_blk
    for c in range(n_blk - 1, -1, -1):
        carries[c] = running
        running = running + tot[c * rows:(c + 1) * rows]
    a = jnp.exp(z + suffix + jnp.concatenate(carries, axis=0)).astype(BF16)
    for c in range(n_blk):
        acc = acc + jnp.dot(a[c * rows:(c + 1) * rows], vp[c * blk:(c + 1) * blk],
                            preferred_element_type=F32)
    g = g_ref[...]
    o = jnp.where(_lo_lanes(), acc[:t_new], acc[t_new:])
    o_ref[...] = (o * (g * _sigmoid(g))).astype(o_ref.dtype)


def _attn_b_sample(q, kb, vb, k_past, v_past, gate, tri):
    b, t_new, _ = q.shape
    past = k_past.shape[1]
    blk = tri.shape[0]
    new_spec = pl.BlockSpec((None, t_new, LANES), lambda bi, h: (bi, 0, h))
    past_spec = pl.BlockSpec((None, past, LANES), lambda bi, h: (bi, 0, h))
    return pl.pallas_call(
        functools.partial(_attn_b_sample_body, t_new=t_new, past=past, blk=blk),
        grid=(b, WIDTH // LANES),
        in_specs=[pl.BlockSpec((blk, blk), lambda bi, h: (0, 0)),
                  new_spec, new_spec, new_spec, past_spec, past_spec, new_spec],
        out_specs=new_spec,
        out_shape=jax.ShapeDtypeStruct((b, t_new, WIDTH), BF16),
        compiler_params=_params(2),
        name="attn_b_sample",
    )(tri, q, kb, vb, k_past, v_past, gate)


def _post_body(*refs, final):
    if final:
        o_ref, x_ref, p_ref, wo_ref, wg_ref, wp_ref, fg_ref, out_ref = refs
    else:
        o_ref, x_ref, p_ref, wo_ref, wg_ref, wp_ref, out_ref = refs
    x = x_ref[...] + jnp.dot(o_ref[...], wo_ref[...], preferred_element_type=F32)
    gate = _sigmoid(jnp.dot(x.astype(BF16), wg_ref[...], preferred_element_type=F32))
    x = x + gate * jnp.dot(p_ref[...].astype(BF16), wp_ref[...], preferred_element_type=F32)
    if final:
        ms = jnp.mean(x * x, axis=-1, keepdims=True)
        x = (x * lax.rsqrt(ms + EPS)) * fg_ref[...]
    out_ref[...] = x


def _post(o, x, p, w_out, w_gate, w_proj, final_g):
    n = x.shape[0]
    rows = PRE_ROWS
    final = final_g is not None
    row_spec = lambda width: pl.BlockSpec((rows, width), lambda i: (i, 0))
    full = lambda shape: pl.BlockSpec(shape, lambda i: (0, 0))
    in_specs = [row_spec(WIDTH), row_spec(D_MODEL), row_spec(PLE_DIM),
                full(w_out.shape), full(w_gate.shape), full(w_proj.shape)]
    args = [o, x, p, w_out, w_gate, w_proj]
    if final:
        in_specs.append(full((1, D_MODEL)))
        args.append(final_g.reshape(1, D_MODEL))
    return pl.pallas_call(
        functools.partial(_post_body, final=final),
        grid=(n // rows,),
        in_specs=in_specs,
        out_specs=row_spec(D_MODEL),
        out_shape=jax.ShapeDtypeStruct((n, D_MODEL), F32),
        compiler_params=_params(1),
        name="post_final" if final else "post",
    )(*args)


def _trunk(x, p, pos, weights, caches):
    (a_norm_g, a_w_in, a_lam, a_subln_g, a_w_out, b_norm_g, b_w_in, b_w_out,
     ple_w_proj, ple_w_gate, final_norm_g) = weights
    b, t, _ = x.shape
    n = b * t
    depth = p.shape[0]
    xf = x.reshape(n, D_MODEL)
    tables = _rope_tables(pos, PRE_ROWS)
    tri = jnp.tri(ATTN_BLK, dtype=BF16)
    ak, av, bk, bv = [], [], [], []
    for i in range(depth):
        j = i // 2
        if i % 2 == 0:
            lam_init = 0.8 - 0.6 * math.exp(-0.3 * i)
            q, k, kb, v, vb, gate = _pre(xf, a_norm_g[j], a_w_in[j], tables)
            shp = (b, t, WIDTH)
            subg = a_subln_g[j].reshape(1, 2 * A_DH)
            if caches is None:
                o = _attn_a_prompt(q.reshape(shp), kb.reshape(shp), vb.reshape(shp),
                                   gate.reshape(shp), a_lam[j], subg, lam_init)
            else:
                past = caches[0].shape[2]
                o = _attn_a_sample(q.reshape(shp), kb.reshape(shp), vb.reshape(shp),
                                   caches[0][j].reshape(b, past, WIDTH),
                                   caches[1][j].reshape(b, past, WIDTH),
                                   gate.reshape(shp), a_lam[j], subg, lam_init)
            ak.append(k.reshape(b, t, A_HEADS, 2, A_DH))
            av.append(v.reshape(b, t, A_HEADS, 2 * A_DH))
            w_out = a_w_out[j]
        else:
            q, k, kb, v, vb, gate = _pre(xf, b_norm_g[j], b_w_in[j], None)
            shp = (b, t, WIDTH)
            if caches is None:
                o = _attn_b_prompt(q.reshape(shp), kb.reshape(shp), vb.reshape(shp),
                                   gate.reshape(shp), tri)
            else:
                past = caches[2].shape[2]
                o = _attn_b_sample(q.reshape(shp), kb.reshape(shp), vb.reshape(shp),
                                   caches[2][j].reshape(b, past, WIDTH),
                                   caches[3][j].reshape(b, past, WIDTH),
                                   gate.reshape(shp), tri)
            bk.append(k.reshape(b, t, B_HEADS, B_DH))
            bv.append(v.reshape(b, t, B_HEADS, B_DH))
            w_out = b_w_out[j]
        xf = _post(o.reshape(n, WIDTH), xf, p[i].reshape(n, PLE_DIM), w_out,
                   ple_w_gate[i], ple_w_proj[i],
                   final_norm_g if i == depth - 1 else None)
    return (xf.reshape(b, t, D_MODEL), jnp.stack(ak), jnp.stack(av), jnp.stack(bk), jnp.stack(bv))


def kernel(x_prompt, x_sample, cache_a_k, cache_a_v, cache_b_k, cache_b_v, p_prompt, p_sample,
           a_norm_g, a_w_in, a_lam_q1, a_lam_k1, a_lam_q2, a_lam_k2, a_subln_g, a_w_out,
           b_norm_g, b_w_in, b_w_out, ple_w_proj, ple_w_gate, final_norm_g):
    past = cache_a_k.shape[2]
    assert past % ATTN_BLK == 0 and x_prompt.shape[1] % ATTN_BLK == 0
    a_lam = jnp.stack([a_lam_q1, a_lam_k1, a_lam_q2, a_lam_k2], axis=1)
    weights = (a_norm_g, a_w_in.astype(BF16), a_lam, a_subln_g, a_w_out.astype(BF16),
               b_norm_g, b_w_in.astype(BF16), b_w_out.astype(BF16),
               ple_w_proj.astype(BF16), ple_w_gate.astype(BF16), final_norm_g)
    pos_p = jnp.arange(x_prompt.shape[1], dtype=jnp.int32)
    pos_s = past + jnp.arange(x_sample.shape[1], dtype=jnp.int32)
    y_p, ak_p, av_p, bk_p, bv_p = _trunk(x_prompt, p_prompt, pos_p, weights, None)
    y_s, ak_s, av_s, bk_s, bv_s = _trunk(x_sample, p_sample, pos_s, weights,
                                         (cache_a_k, cache_a_v, cache_b_k, cache_b_v))
    return (y_p, y_s, ak_p, av_p, bk_p, bv_p, ak_s, av_s, bk_s, bv_s)
```

```python
import functools
import math

import jax
import jax.numpy as jnp
from jax import lax
from jax.experimental import pallas as pl
from jax.experimental.pallas import tpu as pltpu

F32 = jnp.float32
BF16 = jnp.bfloat16

D_MODEL = 1024
CHUNK = 64
PLE_DIM = 256
ROPE_THETA = 500000.0
EPS = 1e-6
NEG_INF = -1e30
A_HEADS = 8
A_DH = 64
A_ROT = A_DH // 4
B_HEADS = 16
B_DH = 64
WIDTH = 1024
LANES = 128
SUBLANES = 8
HALF = 64
Q_SCALE = A_DH ** -0.5 * math.log2(math.e)

PRE_ROWS = 256
KV_TILE = 256
TRI = 256
A_BQ, A_BK = 1024, 512
B_BQ, B_BK = 1024, 512
VMEM_LIMIT = 48 * 1024 * 1024

_NT = (((1,), (1,)), ((), ()))


def _params(n_axes, vmem_limit=VMEM_LIMIT):
    return pltpu.CompilerParams(
        dimension_semantics=("arbitrary",) * n_axes, vmem_limit_bytes=vmem_limit)


def _sigmoid(x):
    return 1.0 / (1.0 + jnp.exp(-x))


def _mod_pow2(x, n):
    assert n & (n - 1) == 0
    return x & (n - 1)


def _chunk_of(pos):
    assert CHUNK & (CHUNK - 1) == 0
    return pos >> (CHUNK.bit_length() - 1)


def _rope_lanes(sl, c_ref, s1_ref, s2_ref):
    return (sl * c_ref[...] + pltpu.roll(sl, LANES - A_ROT // 2, 1) * s1_ref[...]
            + pltpu.roll(sl, A_ROT // 2, 1) * s2_ref[...])


def _normed(x_ref, g_ref):
    x = x_ref[...]
    ms = jnp.mean(x * x, axis=-1, keepdims=True)
    return ((x * lax.rsqrt(ms + EPS)) * g_ref[...]).astype(BF16)


def _pre_rows_body(*refs, rope):
    if rope:
        x_ref, g_ref, wq_ref, wk_ref, wv_ref, wg_ref, c_ref, s1_ref, s2_ref = refs[:9]
        outs = refs[9:]
    else:
        x_ref, g_ref, wq_ref, wk_ref, wv_ref, wg_ref = refs[:6]
        outs = refs[6:]
    q_ref, k_ref, kb_ref, v_ref, vb_ref, gate_ref = outs
    hb = _normed(x_ref, g_ref)

    def rot(r, j):
        sl = r[:, j * LANES:(j + 1) * LANES]
        return _rope_lanes(sl, c_ref, s1_ref, s2_ref) if rope else sl

    q = jnp.dot(hb, wq_ref[...], preferred_element_type=F32)
    for j in range(WIDTH // LANES):
        q_ref[:, j * LANES:(j + 1) * LANES] = (rot(q, j) * Q_SCALE).astype(BF16)
    k = jnp.dot(hb, wk_ref[...], preferred_element_type=F32)
    for j in range(WIDTH // LANES):
        kr = rot(k, j)
        k_ref[:, j * LANES:(j + 1) * LANES] = kr
        kb_ref[:, j * LANES:(j + 1) * LANES] = kr.astype(BF16)
    v = jnp.dot(hb, wv_ref[...], preferred_element_type=F32)
    v_ref[...] = v
    vb_ref[...] = v.astype(BF16)
    gate_ref[...] = jnp.dot(hb, wg_ref[...], preferred_element_type=F32)


def _pre_cols_body(*refs, rope, v_cols):
    if rope:
        x_ref, g_ref, wq_ref, wkt_ref, wv_ref, wg_ref, c_ref, s1_ref, s2_ref, ct_ref, st_ref = refs[:11]
        outs = refs[11:]
    else:
        x_ref, g_ref, wq_ref, wkt_ref, wv_ref, wg_ref = refs[:6]
        outs = refs[6:]
    q_ref, kt_ref, ktb_ref, v_ref, vb_ref, gate_ref = outs
    hb = _normed(x_ref, g_ref)

    q = jnp.dot(hb, wq_ref[...], preferred_element_type=F32)
    for j in range(WIDTH // LANES):
        sl = q[:, j * LANES:(j + 1) * LANES]
        if rope:
            sl = _rope_lanes(sl, c_ref, s1_ref, s2_ref)
        q_ref[:, j * LANES:(j + 1) * LANES] = (sl * Q_SCALE).astype(BF16)

    kt = lax.dot_general(wkt_ref[...], hb, _NT, preferred_element_type=F32)
    if rope:
        half = A_ROT // 2
        assert half == SUBLANES
        cos, sin = ct_ref[...], st_ref[...]
        for grp in range(WIDTH // A_DH):
            r0 = grp * A_DH
            x0, x1 = kt[r0:r0 + half], kt[r0 + half:r0 + 2 * half]
            y0, y1 = x0 * cos - x1 * sin, x1 * cos + x0 * sin
            kt_ref[r0:r0 + half, :] = y0
            kt_ref[r0 + half:r0 + 2 * half, :] = y1
            kt_ref[r0 + 2 * half:r0 + A_DH, :] = kt[r0 + 2 * half:r0 + A_DH]
            ktb_ref[r0:r0 + A_DH, :] = jnp.concatenate(
                [y0, y1, kt[r0 + 2 * half:r0 + A_DH]], axis=0).astype(BF16)
    else:
        kt_ref[...] = kt
        ktb_ref[...] = kt.astype(BF16)

    if v_cols:
        vt = lax.dot_general(wv_ref[...], hb, _NT, preferred_element_type=F32)
        v_ref[...] = vt
        vb_ref[...] = vt.astype(BF16)
    else:
        v = jnp.dot(hb, wv_ref[...], preferred_element_type=F32)
        v_ref[...] = v
        vb_ref[...] = v.astype(BF16)
    gate_ref[...] = jnp.dot(hb, wg_ref[...], preferred_element_type=F32)


def _split_w(w_in, t_k, t_v):
    parts = [w_in[:, i * WIDTH:(i + 1) * WIDTH] for i in range(4)]
    if t_k:
        parts[1] = parts[1].T
    if t_v:
        parts[2] = parts[2].T
    return parts


def _pre_rows(x, norm_g, w_in, tables):
    n = x.shape[0]
    rows = PRE_ROWS
    rope = tables is not None
    row_spec = lambda width: pl.BlockSpec((rows, width), lambda i: (i, 0))
    full = lambda shape: pl.BlockSpec(shape, lambda i: (0, 0))
    in_specs = [row_spec(D_MODEL), full((1, D_MODEL))] + [full((D_MODEL, WIDTH))] * 4
    args = [x, norm_g.reshape(1, D_MODEL)] + _split_w(w_in, False, False)
    if rope:
        in_specs += [full((rows, LANES))] * 3
        args += list(tables)
    out_shape = [jax.ShapeDtypeStruct((n, WIDTH), dt) for dt in (BF16, F32, BF16, F32, BF16, F32)]
    return pl.pallas_call(
        functools.partial(_pre_rows_body, rope=rope),
        grid=(n // rows,),
        in_specs=in_specs,
        out_specs=[row_spec(WIDTH)] * 6,
        out_shape=out_shape,
        compiler_params=_params(1),
        name="pre_rows_rope" if rope else "pre_rows",
    )(*args)


def _pre_cols(x, b, t, norm_g, w_in, tables, v_cols):
    rows = PRE_ROWS
    assert rows == KV_TILE and t % rows == 0
    n_t = t // rows
    rope = tables is not None
    row_spec = lambda width: pl.BlockSpec((rows, width), lambda i: (i, 0))
    full = lambda shape: pl.BlockSpec(shape, lambda i: (0, 0))
    colf_spec = pl.BlockSpec((None, WIDTH, rows), lambda i: (i // n_t, 0, i % n_t))
    colb_spec = pl.BlockSpec((None, None, WIDTH, rows), lambda i: (i // n_t, i % n_t, 0, 0))
    in_specs = [row_spec(D_MODEL), full((1, D_MODEL))] + [full((D_MODEL, WIDTH))] * 4
    args = [x, norm_g.reshape(1, D_MODEL)] + _split_w(w_in, True, v_cols)
    if rope:
        lanes_tab, rows_tab = tables
        in_specs += [pl.BlockSpec((rows, LANES), lambda i: (i % n_t, 0))] * 3
        in_specs += [pl.BlockSpec((A_ROT // 2, rows), lambda i: (0, i % n_t))] * 2
        args += list(lanes_tab) + list(rows_tab)
    colf = jax.ShapeDtypeStruct((b, WIDTH, t), F32)
    colb = jax.ShapeDtypeStruct((b, n_t, WIDTH, rows), BF16)
    rowf = jax.ShapeDtypeStruct((b * t, WIDTH), F32)
    rowb = jax.ShapeDtypeStruct((b * t, WIDTH), BF16)
    out_shape = [rowb, colf, colb] + ([colf, colb] if v_cols else [rowf, rowb]) + [rowf]
    out_specs = ([row_spec(WIDTH), colf_spec, colb_spec]
                 + ([colf_spec, colb_spec] if v_cols else [row_spec(WIDTH)] * 2) + [row_spec(WIDTH)])
    return pl.pallas_call(
        functools.partial(_pre_cols_body, rope=rope, v_cols=v_cols),
        grid=(b * n_t,),
        in_specs=in_specs,
        out_specs=out_specs,
        out_shape=out_shape,
        compiler_params=_params(1),
        name="pre_cols_rope" if rope else "pre_cols",
    )(*args)


def _rope_tables(pos, rows):
    half = A_ROT // 2
    inv = ROPE_THETA ** (-jnp.arange(half, dtype=F32) * 2.0 / A_ROT)
    ang = pos.astype(F32)[:, None] * inv[None, :]
    cos, sin = jnp.cos(ang), jnp.sin(ang)
    t = pos.shape[0]
    c = jnp.concatenate([cos, cos, jnp.ones((t, HALF - A_ROT), F32)], axis=1)
    s1 = jnp.concatenate([-sin, jnp.zeros((t, HALF - half), F32)], axis=1)
    s2 = jnp.concatenate([jnp.zeros((t, half), F32), sin, jnp.zeros((t, HALF - A_ROT), F32)], axis=1)
    reps = (max(rows // t, 1), LANES // HALF)
    return tuple(jnp.tile(a, reps) for a in (c, s1, s2)), (cos.T, sin.T)


def _lo_lanes():
    return lax.broadcasted_iota(jnp.int32, (1, LANES), 1) < HALF


def _halves(q):
    lo = _lo_lanes()
    zero = jnp.zeros_like(q)
    return jnp.where(lo, q, zero), jnp.where(lo, zero, q)


def _lam(lam_ref, lam_init):
    a = jnp.sum(lam_ref[0:1, :] * lam_ref[1:2, :], axis=-1, keepdims=True)
    b = jnp.sum(lam_ref[2:3, :] * lam_ref[3:4, :], axis=-1, keepdims=True)
    return jnp.exp(a) - jnp.exp(b) + lam_init


def _diff_epilogue(o1, o2, lam_ref, subg_ref, g_ref, o_ref, lam_init):
    o = o1 - _lam(lam_ref, lam_init) * o2
    ms = jnp.mean(o * o, axis=-1, keepdims=True)
    o = (o * lax.rsqrt(ms + EPS)) * subg_ref[...] * (1.0 - lam_init)
    g = g_ref[...]
    o_ref[...] = (o * (g * _sigmoid(g))).astype(o_ref.dtype)


def _softplus2(z):
    return jnp.maximum(z, 0.0) + jnp.log2(1.0 + jnp.exp2(-jnp.abs(z)))


def _stick_weights(z, valid, carry, tri):
    sp = _softplus2(z)
    if valid is not None:
        sp = jnp.where(valid, sp, 0.0)
    pieces = [None] * (z.shape[1] // TRI)
    for sb in range(len(pieces) - 1, -1, -1):
        cols = slice(sb * TRI, (sb + 1) * TRI)
        sps = sp[:, cols]
        suffix = jnp.dot(sps.astype(BF16), tri, preferred_element_type=F32)
        pieces[sb] = jnp.exp2(z[:, cols] - suffix - carry)
        carry = carry + suffix[:, :1]
    a = pieces[0] if len(pieces) == 1 else jnp.concatenate(pieces, axis=1)
    if valid is not None:
        a = jnp.where(valid, a, 0.0)
    return a, carry


def _kv_tiles(ref, kj, bk):
    per = bk // KV_TILE
    tiles = [ref[kj * per + i] for i in range(per)]
    return tiles[0] if per == 1 else jnp.concatenate(tiles, axis=1)


def _attn_a_prompt_body(lam_ref, subg_ref, q_ref, kt_ref, v_ref, g_ref, o_ref,
                        qz_ref, m_ref, l_ref, acc_ref, *, bq, bk, lam_init):
    qi = pl.program_id(2)
    q1, q2 = _halves(q_ref[...])
    qz_ref[0] = q1
    qz_ref[1] = q2
    m_ref[...] = jnp.full(m_ref.shape, NEG_INF, F32)
    l_ref[...] = jnp.zeros(l_ref.shape, F32)
    acc_ref[...] = jnp.zeros(acc_ref.shape, F32)

    def step(kj, masked):
        kt = _kv_tiles(kt_ref, kj, bk)
        v = v_ref[pl.ds(pl.multiple_of(kj * bk, bk), bk), :]
        if masked:
            qpos = qi * bq + lax.broadcasted_iota(jnp.int32, (bq, bk), 0)
            kpos = kj * bk + lax.broadcasted_iota(jnp.int32, (bq, bk), 1)
            vis = _chunk_of(kpos) <= _chunk_of(qpos)
        for c in range(2):
            s = jnp.dot(qz_ref[c], kt, preferred_element_type=F32)
            if masked:
                s = jnp.where(vis, s, NEG_INF)
            m_prev = m_ref[c]
            m_new = jnp.maximum(m_prev, jnp.max(s, axis=-1, keepdims=True))
            alpha = jnp.exp2(m_prev - m_new)
            p = jnp.exp2(s - jnp.tile(m_new, (1, bk // LANES)))
            l_ref[c] = alpha * l_ref[c] + jnp.sum(p, axis=-1, keepdims=True)
            acc_ref[c] = alpha * acc_ref[c] + jnp.dot(p.astype(BF16), v, preferred_element_type=F32)
            m_ref[c] = m_new

    n_full = qi * (bq // bk)

    def body(kj, c):
        step(kj, False)
        return c

    lax.fori_loop(0, n_full, body, 0)
    for r in range(bq // bk):
        step(n_full + r, True)

    _diff_epilogue(acc_ref[0] / l_ref[0], acc_ref[1] / l_ref[1],
                   lam_ref, subg_ref, g_ref, o_ref, lam_init)


def _attn_a_prompt(q, ktb, vb, gate, lam4, subg, lam_init):
    b, t, _ = q.shape
    bq, bk = A_BQ, A_BK
    assert bq % bk == 0 and bk % KV_TILE == 0 and t % bq == 0 and bk % CHUNK == 0
    q_spec = pl.BlockSpec((None, bq, LANES), lambda bi, h, qi: (bi, qi, h))
    kt_spec = pl.BlockSpec((None, t // KV_TILE, LANES, KV_TILE), lambda bi, h, qi: (bi, 0, h, 0))
    v_spec = pl.BlockSpec((None, t, LANES), lambda bi, h, qi: (bi, 0, h))
    full = lambda shape: pl.BlockSpec(shape, lambda bi, h, qi: (0,) * len(shape))
    return pl.pallas_call(
        functools.partial(_attn_a_prompt_body, bq=bq, bk=bk, lam_init=lam_init),
        grid=(b, A_HEADS, t // bq),
        in_specs=[full(lam4.shape), full(subg.shape), q_spec, kt_spec, v_spec, q_spec],
        out_specs=q_spec,
        out_shape=jax.ShapeDtypeStruct((b, t, WIDTH), BF16),
        scratch_shapes=[pltpu.VMEM((2, bq, LANES), BF16),
                        pltpu.VMEM((2, bq, LANES), F32),
                        pltpu.VMEM((2, bq, LANES), F32),
                        pltpu.VMEM((2, bq, LANES), F32)],
        compiler_params=_params(3),
        name="attn_a_prompt",
    )(lam4, subg, q, ktb, vb, gate)


def _attn_b_prompt_body(tri_ref, q_ref, kt_ref, vt_ref, g_ref, o_ref,
                        qz_ref, carry_ref, acc_ref, *, bq, bk):
    qi = pl.program_id(2)
    q1, q2 = _halves(q_ref[...])
    qz_ref[0] = q1
    qz_ref[1] = q2
    carry_ref[...] = jnp.zeros(carry_ref.shape, F32)
    acc_ref[...] = jnp.zeros(acc_ref.shape, F32)

    def step(kj, masked):
        kt = _kv_tiles(kt_ref, kj, bk)
        vt = _kv_tiles(vt_ref, kj, bk)
        valid = None
        if masked:
            qpos = qi * bq + lax.broadcasted_iota(jnp.int32, (bq, bk), 0)
            kpos = kj * bk + lax.broadcasted_iota(jnp.int32, (bq, bk), 1)
            valid = kpos < qpos
        for c in range(2):
            z = jnp.dot(qz_ref[c], kt, preferred_element_type=F32)
            a, carry = _stick_weights(z, valid, carry_ref[c], tri_ref[...])
            acc_ref[c] += lax.dot_general(a.astype(BF16), vt, _NT, preferred_element_type=F32)
            carry_ref[c] = carry

    n_full = qi * (bq // bk)
    for r in range(bq // bk - 1, -1, -1):
        step(n_full + r, True)

    def body(t, c):
        step(n_full - 1 - t, False)
        return c

    lax.fori_loop(0, n_full, body, 0)

    g = g_ref[...]
    o = jnp.where(_lo_lanes(), acc_ref[0], acc_ref[1])
    o_ref[...] = (o * (g * _sigmoid(g))).astype(o_ref.dtype)


def _attn_b_prompt(q, ktb, vtb, gate, tri):
    b, t, _ = q.shape
    bq, bk = B_BQ, B_BK
    assert bq % bk == 0 and bk % KV_TILE == 0 and t % bq == 0 and bk % TRI == 0
    q_spec = pl.BlockSpec((None, bq, LANES), lambda bi, h, qi: (bi, qi, h))
    kt_spec = pl.BlockSpec((None, t // KV_TILE, LANES, KV_TILE), lambda bi, h, qi: (bi, 0, h, 0))
    return pl.pallas_call(
        functools.partial(_attn_b_prompt_body, bq=bq, bk=bk),
        grid=(b, WIDTH // LANES, t // bq),
        in_specs=[pl.BlockSpec((TRI, TRI), lambda bi, h, qi: (0, 0)),
                  q_spec, kt_spec, kt_spec, q_spec],
        out_specs=q_spec,
        out_shape=jax.ShapeDtypeStruct((b, t, WIDTH), BF16),
        scratch_shapes=[pltpu.VMEM((2, bq, LANES), BF16),
                        pltpu.VMEM((2, bq, 1), F32),
                        pltpu.VMEM((2, bq, LANES), F32)],
        compiler_params=_params(3),
        name="attn_b_prompt",
    )(tri, q, ktb, vtb, gate)


def _pad_rows(x, rows):
    return jnp.concatenate([x, jnp.zeros((rows - x.shape[0], x.shape[1]), x.dtype)], axis=0)


def _attn_a_sample_body(lam_ref, subg_ref, q_ref, kn_ref, vn_ref, ktp_ref, vp_ref, g_ref, o_ref,
                        *, t_new, past, lam_init):
    qq = jnp.concatenate(_halves(q_ref[...]), axis=0)
    s_p = jnp.dot(qq, ktp_ref[...].astype(BF16), preferred_element_type=F32)
    kn = _pad_rows(kn_ref[...], LANES)
    vn = _pad_rows(vn_ref[...], LANES)
    s_n = lax.dot_general(qq, kn, _NT, preferred_element_type=F32)
    row = lax.broadcasted_iota(jnp.int32, s_n.shape, 0)
    col = lax.broadcasted_iota(jnp.int32, s_n.shape, 1)
    vis = jnp.logical_and(col < t_new,
                          _chunk_of(past + col) <= _chunk_of(past + _mod_pow2(row, t_new)))
    s_n = jnp.where(vis, s_n, NEG_INF)
    m = jnp.maximum(jnp.max(s_p, axis=-1, keepdims=True), jnp.max(s_n, axis=-1, keepdims=True))
    p_p = jnp.exp2(s_p - m)
    p_n = jnp.exp2(s_n - m)
    l = jnp.sum(p_p, axis=-1, keepdims=True) + jnp.sum(p_n, axis=-1, keepdims=True)
    vp = vp_ref[pl.ds(pl.program_id(1), past, stride=A_HEADS), :]
    acc = (jnp.dot(p_p.astype(BF16), vp.astype(BF16), preferred_element_type=F32)
           + jnp.dot(p_n.astype(BF16), vn, preferred_element_type=F32))
    o = acc / l
    _diff_epilogue(o[:t_new], o[t_new:], lam_ref, subg_ref, g_ref, o_ref, lam_init)


def _attn_a_sample(q, kb, vb, kt_past, v_past, gate, lam4, subg, lam_init):
    b, t_new, _ = q.shape
    past = kt_past.shape[2]
    assert v_past.shape[1] == past * A_HEADS
    new_spec = pl.BlockSpec((None, t_new, LANES), lambda bi, h: (bi, 0, h))
    ktp_spec = pl.BlockSpec((None, LANES, past), lambda bi, h: (bi, h, 0))
    vp_spec = pl.BlockSpec((None, past * A_HEADS, LANES), lambda bi, h: (bi, 0, 0))
    full = lambda shape: pl.BlockSpec(shape, lambda bi, h: (0,) * len(shape))
    return pl.pallas_call(
        functools.partial(_attn_a_sample_body, t_new=t_new, past=past, lam_init=lam_init),
        grid=(b, A_HEADS),
        in_specs=[full(lam4.shape), full(subg.shape), new_spec, new_spec, new_spec,
                  ktp_spec, vp_spec, new_spec],
        out_specs=new_spec,
        out_shape=jax.ShapeDtypeStruct((b, t_new, WIDTH), BF16),
        compiler_params=_params(2, 2 * past * WIDTH * 4 + 24 * 1024 * 1024),
        name="attn_a_sample",
    )(lam4, subg, q, kb, vb, kt_past, v_past, gate)


def _attn_b_sample_body(tri_ref, q_ref, kn_ref, vn_ref, ktp_ref, vtp_ref, g_ref, o_ref,
                        *, t_new, past):
    rows = 2 * t_new
    n_blk = past // TRI
    qq = jnp.concatenate(_halves(q_ref[...]), axis=0)
    tri = tri_ref[...]

    kn = _pad_rows(kn_ref[...], TRI)
    vn = _pad_rows(vn_ref[...], TRI)
    z_n = lax.dot_general(qq, kn, _NT, preferred_element_type=F32)
    row = lax.broadcasted_iota(jnp.int32, z_n.shape, 0)
    col = lax.broadcasted_iota(jnp.int32, z_n.shape, 1)
    valid = col < _mod_pow2(row, t_new)
    a_n, running = _stick_weights(z_n, valid, jnp.zeros((rows, 1), F32), tri)
    acc = jnp.dot(a_n.astype(BF16), vn, preferred_element_type=F32)

    z_all = jnp.dot(qq, ktp_ref[...].astype(BF16), preferred_element_type=F32)
    z = jnp.concatenate([z_all[:, c * TRI:(c + 1) * TRI] for c in range(n_blk)], axis=0)
    sp = _softplus2(z)
    suffix = jnp.dot(sp.astype(BF16), tri, preferred_element_type=F32)
    tot = suffix[:, :1]
    carries = [None] * n_blk
    for c in range(n_blk - 1, -1, -1):
        carries[c] = running
        running = running + tot[c * rows:(c + 1) * rows]
    a = jnp.exp2(z - suffix - jnp.concatenate(carries, axis=0)).astype(BF16)
    vtp = vtp_ref[...].astype(BF16)
    for c in range(n_blk):
        acc = acc + lax.dot_general(a[c * rows:(c + 1) * rows], vtp[:, c * TRI:(c + 1) * TRI], _NT,
                                    preferred_element_type=F32)
    g = g_ref[...]
    o = jnp.where(_lo_lanes(), acc[:t_new], acc[t_new:])
    o_ref[...] = (o * (g * _sigmoid(g))).astype(o_ref.dtype)


def _attn_b_sample(q, kb, vb, kt_past, vt_past, gate, tri):
    b, t_new, _ = q.shape
    past = kt_past.shape[2]
    assert past % TRI == 0 and t_new <= TRI
    new_spec = pl.BlockSpec((None, t_new, LANES), lambda bi, h: (bi, 0, h))
    past_spec = pl.BlockSpec((None, LANES, past), lambda bi, h: (bi, h, 0))
    return pl.pallas_call(
        functools.partial(_attn_b_sample_body, t_new=t_new, past=past),
        grid=(b, WIDTH // LANES),
        in_specs=[pl.BlockSpec((TRI, TRI), lambda bi, h: (0, 0)),
                  new_spec, new_spec, new_spec, past_spec, past_spec, new_spec],
        out_specs=new_spec,
        out_shape=jax.ShapeDtypeStruct((b, t_new, WIDTH), BF16),
        compiler_params=_params(2),
        name="attn_b_sample",
    )(tri, q, kb, vb, kt_past, vt_past, gate)


def _post_body(*refs, final):
    if final:
        o_ref, x_ref, p_ref, wo_ref, wg_ref, wp_ref, fg_ref, out_ref = refs
    else:
        o_ref, x_ref, p_ref, wo_ref, wg_ref, wp_ref, out_ref = refs
    x = x_ref[...] + jnp.dot(o_ref[...], wo_ref[...], preferred_element_type=F32)
    gate = _sigmoid(jnp.dot(x.astype(BF16), wg_ref[...], preferred_element_type=F32))
    x = x + gate * jnp.dot(p_ref[...].astype(BF16), wp_ref[...], preferred_element_type=F32)
    if final:
        ms = jnp.mean(x * x, axis=-1, keepdims=True)
        x = (x * lax.rsqrt(ms + EPS)) * fg_ref[...]
    out_ref[...] = x


def _post(o, x, p, w_out, w_gate, w_proj, final_g):
    n = x.shape[0]
    rows = PRE_ROWS
    final = final_g is not None
    row_spec = lambda width: pl.BlockSpec((rows, width), lambda i: (i, 0))
    full = lambda shape: pl.BlockSpec(shape, lambda i: (0, 0))
    in_specs = [row_spec(WIDTH), row_spec(D_MODEL), row_spec(PLE_DIM),
                full(w_out.shape), full(w_gate.shape), full(w_proj.shape)]
    args = [o, x, p, w_out, w_gate, w_proj]
    if final:
        in_specs.append(full((1, D_MODEL)))
        args.append(final_g.reshape(1, D_MODEL))
    return pl.pallas_call(
        functools.partial(_post_body, final=final),
        grid=(n // rows,),
        in_specs=in_specs,
        out_specs=row_spec(D_MODEL),
        out_shape=jax.ShapeDtypeStruct((n, D_MODEL), F32),
        compiler_params=_params(1),
        name="post_final" if final else "post",
    )(*args)


def _mixer_order(depth):
    return [("a", i // 2) if i % 2 == 0 else ("b", i // 2) for i in range(depth)]


def _lam_init(i):
    return 0.8 - 0.6 * math.exp(-0.3 * i)


def _trunk_prompt(x, p, pos, weights):
    (a_norm_g, a_w_in, a_lam, a_subln_g, a_w_out, b_norm_g, b_w_in, b_w_out,
     ple_w_proj, ple_w_gate, final_norm_g) = weights
    b, t, _ = x.shape
    n = b * t
    depth = p.shape[0]
    xf = x.reshape(n, D_MODEL)
    tables = _rope_tables(pos, PRE_ROWS)
    tri = jnp.tri(TRI, dtype=BF16)
    shp = (b, t, WIDTH)
    ak, av, bk, bv = [], [], [], []
    for i, (kind, j) in enumerate(_mixer_order(depth)):
        if kind == "a":
            q, kt, ktb, v, vb, gate = _pre_cols(xf, b, t, a_norm_g[j], a_w_in[j], tables, False)
            o = _attn_a_prompt(q.reshape(shp), ktb, vb.reshape(shp), gate.reshape(shp),
                               a_lam[j], a_subln_g[j].reshape(1, 2 * A_DH), _lam_init(i))
            ak.append(kt.reshape(b, A_HEADS, 2, A_DH, t).transpose(0, 4, 1, 2, 3))
            av.append(v.reshape(b, t, A_HEADS, 2 * A_DH))
            w_out = a_w_out[j]
        else:
            q, kt, ktb, vt, vtb, gate = _pre_cols(xf, b, t, b_norm_g[j], b_w_in[j], None, True)
            o = _attn_b_prompt(q.reshape(shp), ktb, vtb, gate.reshape(shp), tri)
            bk.append(kt.reshape(b, B_HEADS, B_DH, t).transpose(0, 3, 1, 2))
            bv.append(vt.reshape(b, B_HEADS, B_DH, t).transpose(0, 3, 1, 2))
            w_out = b_w_out[j]
        xf = _post(o.reshape(n, WIDTH), xf, p[i].reshape(n, PLE_DIM), w_out,
                   ple_w_gate[i], ple_w_proj[i], final_norm_g if i == depth - 1 else None)
    return (xf.reshape(b, t, D_MODEL), jnp.stack(ak), jnp.stack(av), jnp.stack(bk), jnp.stack(bv))


def _trunk_sample(x, p, pos, weights, caches):
    (a_norm_g, a_w_in, a_lam, a_subln_g, a_w_out, b_norm_g, b_w_in, b_w_out,
     ple_w_proj, ple_w_gate, final_norm_g) = weights
    cache_a_k, cache_a_v, cache_b_k, cache_b_v = caches
    b, t, _ = x.shape
    n = b * t
    depth = p.shape[0]
    past = cache_a_k.shape[2]
    xf = x.reshape(n, D_MODEL)
    lane_tables, _ = _rope_tables(pos, PRE_ROWS)
    tri = jnp.tri(TRI, dtype=BF16)
    shp = (b, t, WIDTH)
    ak, av, bk, bv = [], [], [], []
    for i, (kind, j) in enumerate(_mixer_order(depth)):
        if kind == "a":
            q, k, kb, v, vb, gate = _pre_rows(xf, a_norm_g[j], a_w_in[j], lane_tables)
            kt_past = cache_a_k[j].transpose(0, 2, 3, 4, 1).reshape(b, WIDTH, past)
            v_past = cache_a_v[j].reshape(b, past * A_HEADS, 2 * A_DH)
            o = _attn_a_sample(q.reshape(shp), kb.reshape(shp), vb.reshape(shp), kt_past, v_past,
                               gate.reshape(shp), a_lam[j], a_subln_g[j].reshape(1, 2 * A_DH), _lam_init(i))
            ak.append(k.reshape(b, t, A_HEADS, 2, A_DH))
            av.append(v.reshape(b, t, A_HEADS, 2 * A_DH))
            w_out = a_w_out[j]
        else:
            q, k, kb, v, vb, gate = _pre_rows(xf, b_norm_g[j], b_w_in[j], None)
            kt_past = cache_b_k[j].transpose(0, 2, 3, 1).reshape(b, WIDTH, past)
            vt_past = cache_b_v[j].transpose(0, 2, 3, 1).reshape(b, WIDTH, past)
            o = _attn_b_sample(q.reshape(shp), kb.reshape(shp), vb.reshape(shp), kt_past, vt_past,
                               gate.reshape(shp), tri)
            bk.append(k.reshape(b, t, B_HEADS, B_DH))
            bv.append(v.reshape(b, t, B_HEADS, B_DH))
            w_out = b_w_out[j]
        xf = _post(o.reshape(n, WIDTH), xf, p[i].reshape(n, PLE_DIM), w_out,
                   ple_w_gate[i], ple_w_proj[i], final_norm_g if i == depth - 1 else None)
    return (xf.reshape(b, t, D_MODEL), jnp.stack(ak), jnp.stack(av), jnp.stack(bk), jnp.stack(bv))


def kernel(x_prompt, x_sample, cache_a_k, cache_a_v, cache_b_k, cache_b_v, p_prompt, p_sample,
           a_norm_g, a_w_in, a_lam_q1, a_lam_k1, a_lam_q2, a_lam_k2, a_subln_g, a_w_out,
           b_norm_g, b_w_in, b_w_out, ple_w_proj, ple_w_gate, final_norm_g):
    past = cache_a_k.shape[2]
    a_lam = jnp.stack([a_lam_q1, a_lam_k1, a_lam_q2, a_lam_k2], axis=1)
    weights = (a_norm_g, a_w_in.astype(BF16), a_lam, a_subln_g, a_w_out.astype(BF16),
               b_norm_g, b_w_in.astype(BF16), b_w_out.astype(BF16),
               ple_w_proj.astype(BF16), ple_w_gate.astype(BF16), final_norm_g)
    pos_p = jnp.arange(x_prompt.shape[1], dtype=jnp.int32)
    pos_s = past + jnp.arange(x_sample.shape[1], dtype=jnp.int32)
    y_p, ak_p, av_p, bk_p, bv_p = _trunk_prompt(x_prompt, p_prompt, pos_p, weights)
    y_s, ak_s, av_s, bk_s, bv_s = _trunk_sample(x_sample, p_sample, pos_s, weights,
                                                (cache_a_k, cache_a_v, cache_b_k, cache_b_v))
    return (y_p, y_s, ak_p, av_p, bk_p, bv_p, ak_s, av_s, bk_s, bv_s)
```

```python
import functools
import math

import jax
import jax.numpy as jnp
from jax import lax
from jax.experimental import pallas as pl
from jax.experimental.pallas import tpu as pltpu

F32 = jnp.float32
BF16 = jnp.bfloat16

D_MODEL = 1024
CHUNK = 64
PLE_DIM = 256
ROPE_THETA = 500000.0
EPS = 1e-6
NEG_INF = -1e30
A_HEADS = 8
A_DH = 64
A_ROT = A_DH // 4
B_HEADS = 16
B_DH = 64
WIDTH = 1024
LANES = 128
SUBLANES = 8
HALF = 64
Q_SCALE = A_DH ** -0.5 * math.log2(math.e)

PRE_ROWS = 256
KV_TILE = 256
TRI = 256
A_BQ, A_BK = 2048, 512
B_BQ, B_BK = 2048, 512
A_SLAB, B_SLAB = 512, 512
SAMPLE_CHUNK = 2048
VMEM_LIMIT = 48 * 1024 * 1024

_NT = (((1,), (1,)), ((), ()))


def _params(n_axes, vmem_limit=VMEM_LIMIT):
    return pltpu.CompilerParams(
        dimension_semantics=("arbitrary",) * n_axes, vmem_limit_bytes=vmem_limit)


def _sigmoid(x):
    return 1.0 / (1.0 + jnp.exp(-x))


def _mod_pow2(x, n):
    assert n & (n - 1) == 0
    return x & (n - 1)


def _chunk_of(pos):
    assert CHUNK & (CHUNK - 1) == 0
    return pos >> (CHUNK.bit_length() - 1)


def _rope_lanes(sl, c_ref, s1_ref, s2_ref):
    return (sl * c_ref[...] + pltpu.roll(sl, LANES - A_ROT // 2, 1) * s1_ref[...]
            + pltpu.roll(sl, A_ROT // 2, 1) * s2_ref[...])


def _normed(x_ref, g_ref):
    x = x_ref[...]
    ms = jnp.mean(x * x, axis=-1, keepdims=True)
    return ((x * lax.rsqrt(ms + EPS)) * g_ref[...]).astype(BF16)


def _pre_rows_body(*refs, rope):
    if rope:
        x_ref, g_ref, wq_ref, wk_ref, wv_ref, wg_ref, c_ref, s1_ref, s2_ref = refs[:9]
        outs = refs[9:]
    else:
        x_ref, g_ref, wq_ref, wk_ref, wv_ref, wg_ref = refs[:6]
        outs = refs[6:]
    q_ref, k_ref, kb_ref, v_ref, vb_ref, gate_ref = outs
    hb = _normed(x_ref, g_ref)

    def rot(r, j):
        sl = r[:, j * LANES:(j + 1) * LANES]
        return _rope_lanes(sl, c_ref, s1_ref, s2_ref) if rope else sl

    q = jnp.dot(hb, wq_ref[...], preferred_element_type=F32)
    for j in range(WIDTH // LANES):
        q_ref[:, j * LANES:(j + 1) * LANES] = (rot(q, j) * Q_SCALE).astype(BF16)
    k = jnp.dot(hb, wk_ref[...], preferred_element_type=F32)
    for j in range(WIDTH // LANES):
        kr = rot(k, j)
        k_ref[:, j * LANES:(j + 1) * LANES] = kr
        kb_ref[:, j * LANES:(j + 1) * LANES] = kr.astype(BF16)
    v = jnp.dot(hb, wv_ref[...], preferred_element_type=F32)
    v_ref[...] = v
    vb_ref[...] = v.astype(BF16)
    gate_ref[...] = jnp.dot(hb, wg_ref[...], preferred_element_type=F32)


def _pre_cols_body(*refs, rope, v_cols):
    if rope:
        x_ref, g_ref, wq_ref, wkt_ref, wv_ref, wg_ref, c_ref, s1_ref, s2_ref, ct_ref, st_ref = refs[:11]
        outs = refs[11:]
    else:
        x_ref, g_ref, wq_ref, wkt_ref, wv_ref, wg_ref = refs[:6]
        outs = refs[6:]
    q_ref, kt_ref, ktb_ref, v_ref, vb_ref, gate_ref = outs
    hb = _normed(x_ref, g_ref)

    q = jnp.dot(hb, wq_ref[...], preferred_element_type=F32)
    for j in range(WIDTH // LANES):
        sl = q[:, j * LANES:(j + 1) * LANES]
        if rope:
            sl = _rope_lanes(sl, c_ref, s1_ref, s2_ref)
        q_ref[:, j * LANES:(j + 1) * LANES] = (sl * Q_SCALE).astype(BF16)

    kt = lax.dot_general(wkt_ref[...], hb, _NT, preferred_element_type=F32)
    if rope:
        half = A_ROT // 2
        assert half == SUBLANES
        cos, sin = ct_ref[...], st_ref[...]
        for grp in range(WIDTH // A_DH):
            r0 = grp * A_DH
            x0, x1 = kt[r0:r0 + half], kt[r0 + half:r0 + 2 * half]
            y0, y1 = x0 * cos - x1 * sin, x1 * cos + x0 * sin
            kt_ref[r0:r0 + half, :] = y0
            kt_ref[r0 + half:r0 + 2 * half, :] = y1
            kt_ref[r0 + 2 * half:r0 + A_DH, :] = kt[r0 + 2 * half:r0 + A_DH]
            ktb_ref[r0:r0 + A_DH, :] = jnp.concatenate(
                [y0, y1, kt[r0 + 2 * half:r0 + A_DH]], axis=0).astype(BF16)
    else:
        kt_ref[...] = kt
        ktb_ref[...] = kt.astype(BF16)

    if v_cols:
        vt = lax.dot_general(wv_ref[...], hb, _NT, preferred_element_type=F32)
        v_ref[...] = vt
        vb_ref[...] = vt.astype(BF16)
    else:
        v = jnp.dot(hb, wv_ref[...], preferred_element_type=F32)
        v_ref[...] = v
        vb_ref[...] = v.astype(BF16)
    gate_ref[...] = jnp.dot(hb, wg_ref[...], preferred_element_type=F32)


def _split_w(w_in, t_k, t_v):
    parts = [w_in[:, i * WIDTH:(i + 1) * WIDTH] for i in range(4)]
    if t_k:
        parts[1] = parts[1].T
    if t_v:
        parts[2] = parts[2].T
    return parts


def _pre_rows(x, norm_g, w_in, tables):
    n = x.shape[0]
    rows = PRE_ROWS
    rope = tables is not None
    row_spec = lambda width: pl.BlockSpec((rows, width), lambda i: (i, 0))
    full = lambda shape: pl.BlockSpec(shape, lambda i: (0, 0))
    in_specs = [row_spec(D_MODEL), full((1, D_MODEL))] + [full((D_MODEL, WIDTH))] * 4
    args = [x, norm_g.reshape(1, D_MODEL)] + _split_w(w_in, False, False)
    if rope:
        in_specs += [full((rows, LANES))] * 3
        args += list(tables)
    out_shape = [jax.ShapeDtypeStruct((n, WIDTH), dt) for dt in (BF16, F32, BF16, F32, BF16, F32)]
    return pl.pallas_call(
        functools.partial(_pre_rows_body, rope=rope),
        grid=(n // rows,),
        in_specs=in_specs,
        out_specs=[row_spec(WIDTH)] * 6,
        out_shape=out_shape,
        compiler_params=_params(1),
        name="pre_rows_rope" if rope else "pre_rows",
    )(*args)


def _pre_cols(x, b, t, norm_g, w_in, tables, v_cols):
    rows = PRE_ROWS
    assert rows == KV_TILE and t % rows == 0
    n_t = t // rows
    rope = tables is not None
    row_spec = lambda width: pl.BlockSpec((rows, width), lambda i: (i, 0))
    full = lambda shape: pl.BlockSpec(shape, lambda i: (0, 0))
    colf_spec = pl.BlockSpec((None, WIDTH, rows), lambda i: (i // n_t, 0, i % n_t))
    colb_spec = pl.BlockSpec((None, None, WIDTH, rows), lambda i: (i // n_t, i % n_t, 0, 0))
    in_specs = [row_spec(D_MODEL), full((1, D_MODEL))] + [full((D_MODEL, WIDTH))] * 4
    args = [x, norm_g.reshape(1, D_MODEL)] + _split_w(w_in, True, v_cols)
    if rope:
        lanes_tab, rows_tab = tables
        in_specs += [pl.BlockSpec((rows, LANES), lambda i: (i % n_t, 0))] * 3
        in_specs += [pl.BlockSpec((A_ROT // 2, rows), lambda i: (0, i % n_t))] * 2
        args += list(lanes_tab) + list(rows_tab)
    colf = jax.ShapeDtypeStruct((b, WIDTH, t), F32)
    colb = jax.ShapeDtypeStruct((b, n_t, WIDTH, rows), BF16)
    rowf = jax.ShapeDtypeStruct((b * t, WIDTH), F32)
    rowb = jax.ShapeDtypeStruct((b * t, WIDTH), BF16)
    out_shape = [rowb, colf, colb] + ([colf, colb] if v_cols else [rowf, rowb]) + [rowf]
    out_specs = ([row_spec(WIDTH), colf_spec, colb_spec]
                 + ([colf_spec, colb_spec] if v_cols else [row_spec(WIDTH)] * 2) + [row_spec(WIDTH)])
    return pl.pallas_call(
        functools.partial(_pre_cols_body, rope=rope, v_cols=v_cols),
        grid=(b * n_t,),
        in_specs=in_specs,
        out_specs=out_specs,
        out_shape=out_shape,
        compiler_params=_params(1),
        name="pre_cols_rope" if rope else "pre_cols",
    )(*args)


def _rope_tables(pos, rows):
    half = A_ROT // 2
    inv = ROPE_THETA ** (-jnp.arange(half, dtype=F32) * 2.0 / A_ROT)
    ang = pos.astype(F32)[:, None] * inv[None, :]
    cos, sin = jnp.cos(ang), jnp.sin(ang)
    t = pos.shape[0]
    c = jnp.concatenate([cos, cos, jnp.ones((t, HALF - A_ROT), F32)], axis=1)
    s1 = jnp.concatenate([-sin, jnp.zeros((t, HALF - half), F32)], axis=1)
    s2 = jnp.concatenate([jnp.zeros((t, half), F32), sin, jnp.zeros((t, HALF - A_ROT), F32)], axis=1)
    reps = (max(rows // t, 1), LANES // HALF)
    return tuple(jnp.tile(a, reps) for a in (c, s1, s2)), (cos.T, sin.T)


def _lo_lanes():
    return lax.broadcasted_iota(jnp.int32, (1, LANES), 1) < HALF


def _halves(q):
    lo = _lo_lanes()
    zero = jnp.zeros_like(q)
    return jnp.where(lo, q, zero), jnp.where(lo, zero, q)


def _lam(lam_ref, lam_init):
    a = jnp.sum(lam_ref[0:1, :] * lam_ref[1:2, :], axis=-1, keepdims=True)
    b = jnp.sum(lam_ref[2:3, :] * lam_ref[3:4, :], axis=-1, keepdims=True)
    return jnp.exp(a) - jnp.exp(b) + lam_init


def _diff_epilogue(o1, o2, lam_ref, subg_ref, g_ref, o_ref, lam_init):
    o = o1 - _lam(lam_ref, lam_init) * o2
    ms = jnp.mean(o * o, axis=-1, keepdims=True)
    o = (o * lax.rsqrt(ms + EPS)) * subg_ref[...] * (1.0 - lam_init)
    g = g_ref[...]
    o_ref[...] = (o * (g * _sigmoid(g))).astype(o_ref.dtype)


def _softplus2(z):
    sign_bit = jnp.uint32(1 << 31)
    neg_abs = lax.bitcast_convert_type(lax.bitcast_convert_type(z, jnp.uint32) | sign_bit, F32)
    return jnp.maximum(z, 0.0) + jnp.log2(1.0 + jnp.exp2(neg_abs))


def _stick_weights(z, valid, carry, tri):
    sp = _softplus2(z)
    if valid is not None:
        sp = jnp.where(valid, sp, 0.0)
    pieces = [None] * (z.shape[1] // TRI)
    for sb in range(len(pieces) - 1, -1, -1):
        cols = slice(sb * TRI, (sb + 1) * TRI)
        sps = sp[:, cols]
        suffix = jnp.dot(sps.astype(BF16), tri, preferred_element_type=F32)
        pieces[sb] = jnp.exp2(z[:, cols] - suffix - carry)
        carry = carry + suffix[:, :1]
    a = pieces[0] if len(pieces) == 1 else jnp.concatenate(pieces, axis=1)
    if valid is not None:
        a = jnp.where(valid, a, 0.0)
    return a, carry


def _kv_tiles(ref, kj, bk):
    per = bk // KV_TILE
    tiles = [ref[kj * per + i] for i in range(per)]
    return tiles[0] if per == 1 else jnp.concatenate(tiles, axis=1)


def _attn_a_prompt_body(lam_ref, subg_ref, q_ref, kt_ref, v_ref, g_ref, o_ref,
                        qz_ref, m_ref, l_ref, acc_ref, *, bq, bk, diag_slab, lam_init):
    qi = pl.program_id(2)
    q1, q2 = _halves(q_ref[...])
    qz_ref[0] = q1
    qz_ref[1] = q2
    m_ref[...] = jnp.full(m_ref.shape, NEG_INF, F32)
    l_ref[...] = jnp.zeros(l_ref.shape, F32)
    acc_ref[...] = jnp.zeros(acc_ref.shape, F32)

    def step(kj, diag):
        kt = _kv_tiles(kt_ref, kj, bk)
        v = v_ref[pl.ds(pl.multiple_of(kj * bk, bk), bk), :]
        slab = bq if diag is None else diag_slab
        for r0 in range(0, bq, slab):
            if diag is None or _chunk_of((diag + 1) * bk - 1) <= _chunk_of(r0):
                vis = None
            elif _chunk_of(diag * bk) > _chunk_of(r0 + slab - 1):
                continue
            else:
                qpos = r0 + lax.broadcasted_iota(jnp.int32, (slab, bk), 0)
                kpos = diag * bk + lax.broadcasted_iota(jnp.int32, (slab, bk), 1)
                vis = _chunk_of(kpos) <= _chunk_of(qpos)
            rows = slice(r0, r0 + slab)
            for c in range(2):
                s = jnp.dot(qz_ref[c, rows, :], kt, preferred_element_type=F32)
                if vis is not None:
                    s = jnp.where(vis, s, NEG_INF)
                m_prev = m_ref[c, rows, :]
                m_new = jnp.maximum(m_prev, jnp.max(s, axis=-1, keepdims=True))
                alpha = jnp.exp2(m_prev - m_new)
                p = jnp.exp2(s - jnp.tile(m_new, (1, bk // LANES)))
                l_ref[c, rows, :] = alpha * l_ref[c, rows, :] + jnp.sum(p, axis=-1, keepdims=True)
                acc_ref[c, rows, :] = (alpha * acc_ref[c, rows, :]
                                       + jnp.dot(p.astype(BF16), v, preferred_element_type=F32))
                m_ref[c, rows, :] = m_new

    n_full = qi * (bq // bk)

    def body(kj, c):
        step(kj, None)
        return c

    lax.fori_loop(0, n_full, body, 0)
    for r in range(bq // bk):
        step(n_full + r, r)

    _diff_epilogue(acc_ref[0] / l_ref[0], acc_ref[1] / l_ref[1],
                   lam_ref, subg_ref, g_ref, o_ref, lam_init)


def _attn_a_prompt(q, ktb, vb, gate, lam4, subg, lam_init):
    b, t, _ = q.shape
    bq, bk = A_BQ, A_BK
    assert bq % bk == 0 and bk % KV_TILE == 0 and t % bq == 0 and bk % CHUNK == 0
    q_spec = pl.BlockSpec((None, bq, LANES), lambda bi, h, qi: (bi, qi, h))
    kt_spec = pl.BlockSpec((None, t // KV_TILE, LANES, KV_TILE), lambda bi, h, qi: (bi, 0, h, 0))
    v_spec = pl.BlockSpec((None, t, LANES), lambda bi, h, qi: (bi, 0, h))
    full = lambda shape: pl.BlockSpec(shape, lambda bi, h, qi: (0,) * len(shape))
    return pl.pallas_call(
        functools.partial(_attn_a_prompt_body, bq=bq, bk=bk, diag_slab=A_SLAB, lam_init=lam_init),
        grid=(b, A_HEADS, t // bq),
        in_specs=[full(lam4.shape), full(subg.shape), q_spec, kt_spec, v_spec, q_spec],
        out_specs=q_spec,
        out_shape=jax.ShapeDtypeStruct((b, t, WIDTH), BF16),
        scratch_shapes=[pltpu.VMEM((2, bq, LANES), BF16),
                        pltpu.VMEM((2, bq, LANES), F32),
                        pltpu.VMEM((2, bq, LANES), F32),
                        pltpu.VMEM((2, bq, LANES), F32)],
        compiler_params=_params(3),
        name="attn_a_prompt",
    )(lam4, subg, q, ktb, vb, gate)


def _attn_b_prompt_body(tri_ref, q_ref, kt_ref, vt_ref, g_ref, o_ref,
                        qz_ref, carry_ref, acc_ref, *, bq, bk, diag_slab):
    qi = pl.program_id(2)
    q1, q2 = _halves(q_ref[...])
    qz_ref[0] = q1
    qz_ref[1] = q2
    carry_ref[...] = jnp.zeros(carry_ref.shape, F32)
    acc_ref[...] = jnp.zeros(acc_ref.shape, F32)

    def step(kj, diag):
        kt = _kv_tiles(kt_ref, kj, bk)
        vt = _kv_tiles(vt_ref, kj, bk)
        slab = bq if diag is None else diag_slab
        for r0 in range(0, bq, slab):
            if diag is None or (diag + 1) * bk <= r0:
                valid = None
            elif diag * bk >= r0 + slab - 1:
                continue
            else:
                qpos = r0 + lax.broadcasted_iota(jnp.int32, (slab, bk), 0)
                kpos = diag * bk + lax.broadcasted_iota(jnp.int32, (slab, bk), 1)
                valid = kpos < qpos
            rows = slice(r0, r0 + slab)
            zs = [jnp.dot(qz_ref[c, rows, :], kt, preferred_element_type=F32) for c in range(2)]
            sps = [_softplus2(z) for z in zs]
            if valid is not None:
                sps = [jnp.where(valid, sp, 0.0) for sp in sps]
            carries = [carry_ref[c, rows, :] for c in range(2)]
            pieces = [[None] * (bk // TRI) for _ in range(2)]
            for sb in range(bk // TRI - 1, -1, -1):
                cols = slice(sb * TRI, (sb + 1) * TRI)
                sfx = [jnp.dot(sp[:, cols].astype(BF16), tri_ref[...], preferred_element_type=F32) for sp in sps]
                for c in range(2):
                    pieces[c][sb] = jnp.exp2(zs[c][:, cols] - sfx[c] - carries[c])
                    carries[c] = carries[c] + sfx[c][:, :1]
            for c in range(2):
                a = jnp.concatenate(pieces[c], axis=1)
                if valid is not None:
                    a = jnp.where(valid, a, 0.0)
                acc_ref[c, rows, :] += lax.dot_general(a.astype(BF16), vt, _NT, preferred_element_type=F32)
                carry_ref[c, rows, :] = carries[c]

    n_full = qi * (bq // bk)
    for r in range(bq // bk - 1, -1, -1):
        step(n_full + r, r)

    def body(t, c):
        step(n_full - 1 - t, None)
        return c

    lax.fori_loop(0, n_full, body, 0)

    g = g_ref[...]
    o = jnp.where(_lo_lanes(), acc_ref[0], acc_ref[1])
    o_ref[...] = (o * (g * _sigmoid(g))).astype(o_ref.dtype)


def _attn_b_prompt(q, ktb, vtb, gate, tri):
    b, t, _ = q.shape
    bq, bk = B_BQ, B_BK
    assert bq % bk == 0 and bk % KV_TILE == 0 and t % bq == 0 and bk % TRI == 0
    q_spec = pl.BlockSpec((None, bq, LANES), lambda bi, h, qi: (bi, qi, h))
    kt_spec = pl.BlockSpec((None, t // KV_TILE, LANES, KV_TILE), lambda bi, h, qi: (bi, 0, h, 0))
    return pl.pallas_call(
        functools.partial(_attn_b_prompt_body, bq=bq, bk=bk, diag_slab=B_SLAB),
        grid=(b, WIDTH // LANES, t // bq),
        in_specs=[pl.BlockSpec((TRI, TRI), lambda bi, h, qi: (0, 0)),
                  q_spec, kt_spec, kt_spec, q_spec],
        out_specs=q_spec,
        out_shape=jax.ShapeDtypeStruct((b, t, WIDTH), BF16),
        scratch_shapes=[pltpu.VMEM((2, bq, LANES), BF16),
                        pltpu.VMEM((2, bq, 1), F32),
                        pltpu.VMEM((2, bq, LANES), F32)],
        compiler_params=_params(3),
        name="attn_b_prompt",
    )(tri, q, ktb, vtb, gate)


def _pad_rows(x, rows):
    return jnp.concatenate([x, jnp.zeros((rows - x.shape[0], x.shape[1]), x.dtype)], axis=0)


def _attn_a_sample_body(lam_ref, subg_ref, q_ref, kn_ref, vn_ref, ktp_ref, vp_ref, g_ref, o_ref,
                        qq_ref, m_ref, l_ref, acc_ref, *, t_new, past, chunk, lam_init):
    pc = pl.program_id(1)

    @pl.when(pc == 0)
    def _():
        for h in range(A_HEADS):
            qq_ref[h] = jnp.concatenate(_halves(q_ref[:, h * LANES:(h + 1) * LANES]), axis=0)
        m_ref[...] = jnp.full(m_ref.shape, NEG_INF, F32)
        l_ref[...] = jnp.zeros(l_ref.shape, F32)
        acc_ref[...] = jnp.zeros(acc_ref.shape, F32)

    def update(h, s, v):
        m_prev = m_ref[h]
        m_new = jnp.maximum(m_prev, jnp.max(s, axis=-1, keepdims=True))
        alpha = jnp.exp2(m_prev - m_new)
        p = jnp.exp2(s - jnp.tile(m_new, (1, s.shape[1] // LANES)))
        l_ref[h] = alpha * l_ref[h] + jnp.sum(p, axis=-1, keepdims=True)
        acc_ref[h] = alpha * acc_ref[h] + jnp.dot(p.astype(BF16), v, preferred_element_type=F32)
        m_ref[h] = m_new

    for h in range(A_HEADS):
        kt = ktp_ref[h * LANES:(h + 1) * LANES, :].astype(BF16)
        v = vp_ref[pl.ds(h, chunk, stride=A_HEADS), :].astype(BF16)
        update(h, jnp.dot(qq_ref[h], kt, preferred_element_type=F32), v)

    @pl.when(pc == pl.num_programs(1) - 1)
    def _():
        row = lax.broadcasted_iota(jnp.int32, (2 * t_new, LANES), 0)
        col = lax.broadcasted_iota(jnp.int32, (2 * t_new, LANES), 1)
        vis = jnp.logical_and(col < t_new,
                              _chunk_of(past + col) <= _chunk_of(past + _mod_pow2(row, t_new)))
        for h in range(A_HEADS):
            cols = slice(h * LANES, (h + 1) * LANES)
            kn = _pad_rows(kn_ref[:, cols], LANES)
            vn = _pad_rows(vn_ref[:, cols], LANES)
            s_n = lax.dot_general(qq_ref[h], kn, _NT, preferred_element_type=F32)
            update(h, jnp.where(vis, s_n, NEG_INF), vn)
            o = acc_ref[h] / l_ref[h]
            _diff_epilogue(o[:t_new], o[t_new:], lam_ref, subg_ref, g_ref.at[:, cols], o_ref.at[:, cols],
                           lam_init)


def _attn_a_sample(q, kb, vb, kt_past, v_past, gate, lam4, subg, lam_init):
    b, t_new, _ = q.shape
    past = kt_past.shape[2]
    chunk = SAMPLE_CHUNK
    assert v_past.shape[1] == past * A_HEADS and past % chunk == 0
    new_spec = pl.BlockSpec((None, t_new, WIDTH), lambda bi, pc: (bi, 0, 0))
    ktp_spec = pl.BlockSpec((None, WIDTH, chunk), lambda bi, pc: (bi, 0, pc))
    vp_spec = pl.BlockSpec((None, chunk * A_HEADS, LANES), lambda bi, pc: (bi, pc, 0))
    full = lambda shape: pl.BlockSpec(shape, lambda bi, pc: (0,) * len(shape))
    state = pltpu.VMEM((A_HEADS, 2 * t_new, LANES), F32)
    return pl.pallas_call(
        functools.partial(_attn_a_sample_body, t_new=t_new, past=past, chunk=chunk, lam_init=lam_init),
        grid=(b, past // chunk),
        in_specs=[full(lam4.shape), full(subg.shape), new_spec, new_spec, new_spec,
                  ktp_spec, vp_spec, new_spec],
        out_specs=new_spec,
        out_shape=jax.ShapeDtypeStruct((b, t_new, WIDTH), BF16),
        scratch_shapes=[pltpu.VMEM((A_HEADS, 2 * t_new, LANES), BF16), state, state, state],
        compiler_params=_params(2),
        name="attn_a_sample",
    )(lam4, subg, q, kb, vb, kt_past, v_past, gate)


def _attn_b_sample_body(tri_ref, q_ref, kn_ref, vn_ref, ktp_ref, vtp_ref, g_ref, o_ref,
                        qq_ref, carry_ref, acc_ref, *, t_new, chunk):
    pc = pl.program_id(1)
    rows = 2 * t_new
    n_blk = chunk // TRI
    n_pairs = WIDTH // LANES
    tri = tri_ref[...]

    @pl.when(pc == 0)
    def _():
        row = lax.broadcasted_iota(jnp.int32, (rows, TRI), 0)
        col = lax.broadcasted_iota(jnp.int32, (rows, TRI), 1)
        valid = col < _mod_pow2(row, t_new)
        for hp in range(n_pairs):
            cols = slice(hp * LANES, (hp + 1) * LANES)
            qq = jnp.concatenate(_halves(q_ref[:, cols]), axis=0)
            qq_ref[hp] = qq
            kn = _pad_rows(kn_ref[:, cols], TRI)
            vn = _pad_rows(vn_ref[:, cols], TRI)
            z_n = lax.dot_general(qq, kn, _NT, preferred_element_type=F32)
            a_n, carry = _stick_weights(z_n, valid, jnp.zeros((rows, 1), F32), tri)
            acc_ref[hp] = jnp.dot(a_n.astype(BF16), vn, preferred_element_type=F32)
            carry_ref[hp] = carry

    for hp in range(n_pairs):
        feat = slice(hp * LANES, (hp + 1) * LANES)
        z_all = jnp.dot(qq_ref[hp], ktp_ref[feat, :].astype(BF16), preferred_element_type=F32)
        z = jnp.concatenate([z_all[:, c * TRI:(c + 1) * TRI] for c in range(n_blk)], axis=0)
        suffix = jnp.dot(_softplus2(z).astype(BF16), tri, preferred_element_type=F32)
        running = carry_ref[hp]
        carries = [None] * n_blk
        for c in range(n_blk - 1, -1, -1):
            carries[c] = running
            running = running + suffix[c * rows:(c + 1) * rows, :1]
        carry_ref[hp] = running
        a = jnp.exp2(z - suffix - jnp.concatenate(carries, axis=0)).astype(BF16)
        vtp = vtp_ref[feat, :].astype(BF16)
        acc = acc_ref[hp]
        for c in range(n_blk):
            acc = acc + lax.dot_general(a[c * rows:(c + 1) * rows], vtp[:, c * TRI:(c + 1) * TRI], _NT,
                                        preferred_element_type=F32)
        acc_ref[hp] = acc

    @pl.when(pc == pl.num_programs(1) - 1)
    def _():
        for hp in range(n_pairs):
            cols = slice(hp * LANES, (hp + 1) * LANES)
            acc = acc_ref[hp]
            g = g_ref[:, cols]
            o = jnp.where(_lo_lanes(), acc[:t_new], acc[t_new:])
            o_ref[:, cols] = (o * (g * _sigmoid(g))).astype(o_ref.dtype)


def _attn_b_sample(q, kb, vb, kt_past, vt_past, gate, tri):
    b, t_new, _ = q.shape
    past = kt_past.shape[2]
    chunk = SAMPLE_CHUNK
    assert past % chunk == 0 and chunk % TRI == 0 and t_new <= TRI
    n_pc = past // chunk
    n_pairs = WIDTH // LANES
    new_spec = pl.BlockSpec((None, t_new, WIDTH), lambda bi, pc: (bi, 0, 0))
    past_spec = pl.BlockSpec((None, WIDTH, chunk), lambda bi, pc: (bi, 0, n_pc - 1 - pc))
    return pl.pallas_call(
        functools.partial(_attn_b_sample_body, t_new=t_new, chunk=chunk),
        grid=(b, n_pc),
        in_specs=[pl.BlockSpec((TRI, TRI), lambda bi, pc: (0, 0)),
                  new_spec, new_spec, new_spec, past_spec, past_spec, new_spec],
        out_specs=new_spec,
        out_shape=jax.ShapeDtypeStruct((b, t_new, WIDTH), BF16),
        scratch_shapes=[pltpu.VMEM((n_pairs, 2 * t_new, LANES), BF16),
                        pltpu.VMEM((n_pairs, 2 * t_new, 1), F32),
                        pltpu.VMEM((n_pairs, 2 * t_new, LANES), F32)],
        compiler_params=_params(2),
        name="attn_b_sample",
    )(tri, q, kb, vb, kt_past, vt_past, gate)


def _post_body(*refs, final):
    if final:
        o_ref, x_ref, p_ref, wo_ref, wg_ref, wp_ref, fg_ref, out_ref = refs
    else:
        o_ref, x_ref, p_ref, wo_ref, wg_ref, wp_ref, out_ref = refs
    x = x_ref[...] + jnp.dot(o_ref[...], wo_ref[...], preferred_element_type=F32)
    gate = _sigmoid(jnp.dot(x.astype(BF16), wg_ref[...], preferred_element_type=F32))
    x = x + gate * jnp.dot(p_ref[...].astype(BF16), wp_ref[...], preferred_element_type=F32)
    if final:
        ms = jnp.mean(x * x, axis=-1, keepdims=True)
        x = (x * lax.rsqrt(ms + EPS)) * fg_ref[...]
    out_ref[...] = x


def _post(o, x, p, w_out, w_gate, w_proj, final_g):
    n = x.shape[0]
    rows = PRE_ROWS
    final = final_g is not None
    row_spec = lambda width: pl.BlockSpec((rows, width), lambda i: (i, 0))
    full = lambda shape: pl.BlockSpec(shape, lambda i: (0, 0))
    in_specs = [row_spec(WIDTH), row_spec(D_MODEL), row_spec(PLE_DIM),
                full(w_out.shape), full(w_gate.shape), full(w_proj.shape)]
    args = [o, x, p, w_out, w_gate, w_proj]
    if final:
        in_specs.append(full((1, D_MODEL)))
        args.append(final_g.reshape(1, D_MODEL))
    return pl.pallas_call(
        functools.partial(_post_body, final=final),
        grid=(n // rows,),
        in_specs=in_specs,
        out_specs=row_spec(D_MODEL),
        out_shape=jax.ShapeDtypeStruct((n, D_MODEL), F32),
        compiler_params=_params(1),
        name="post_final" if final else "post",
    )(*args)


def _mixer_order(depth):
    return [("a", i // 2) if i % 2 == 0 else ("b", i // 2) for i in range(depth)]


def _lam_init(i):
    return 0.8 - 0.6 * math.exp(-0.3 * i)


def _trunk_prompt(x, p, pos, weights):
    (a_norm_g, a_w_in, a_lam, a_subln_g, a_w_out, b_norm_g, b_w_in, b_w_out,
     ple_w_proj, ple_w_gate, final_norm_g) = weights
    b, t, _ = x.shape
    n = b * t
    depth = p.shape[0]
    xf = x.reshape(n, D_MODEL)
    tables = _rope_tables(pos, PRE_ROWS)
    tri = jnp.tri(TRI, dtype=BF16)
    shp = (b, t, WIDTH)
    ak, av, bk, bv = [], [], [], []
    for i, (kind, j) in enumerate(_mixer_order(depth)):
        if kind == "a":
            q, kt, ktb, v, vb, gate = _pre_cols(xf, b, t, a_norm_g[j], a_w_in[j], tables, False)
            o = _attn_a_prompt(q.reshape(shp), ktb, vb.reshape(shp), gate.reshape(shp),
                               a_lam[j], a_subln_g[j].reshape(1, 2 * A_DH), _lam_init(i))
            ak.append(kt.reshape(b, A_HEADS, 2, A_DH, t).transpose(0, 4, 1, 2, 3))
            av.append(v.reshape(b, t, A_HEADS, 2 * A_DH))
            w_out = a_w_out[j]
        else:
            q, kt, ktb, vt, vtb, gate = _pre_cols(xf, b, t, b_norm_g[j], b_w_in[j], None, True)
            o = _attn_b_prompt(q.reshape(shp), ktb, vtb, gate.reshape(shp), tri)
            bk.append(kt.reshape(b, B_HEADS, B_DH, t).transpose(0, 3, 1, 2))
            bv.append(vt.reshape(b, B_HEADS, B_DH, t).transpose(0, 3, 1, 2))
            w_out = b_w_out[j]
        xf = _post(o.reshape(n, WIDTH), xf, p[i].reshape(n, PLE_DIM), w_out,
                   ple_w_gate[i], ple_w_proj[i], final_norm_g if i == depth - 1 else None)
    return (xf.reshape(b, t, D_MODEL), jnp.stack(ak), jnp.stack(av), jnp.stack(bk), jnp.stack(bv))


def _trunk_sample(x, p, pos, weights, caches):
    (a_norm_g, a_w_in, a_lam, a_subln_g, a_w_out, b_norm_g, b_w_in, b_w_out,
     ple_w_proj, ple_w_gate, final_norm_g) = weights
    cache_a_k, cache_a_v, cache_b_k, cache_b_v = caches
    b, t, _ = x.shape
    n = b * t
    depth = p.shape[0]
    past = cache_a_k.shape[2]
    xf = x.reshape(n, D_MODEL)
    lane_tables, _ = _rope_tables(pos, PRE_ROWS)
    tri = jnp.tri(TRI, dtype=BF16)
    shp = (b, t, WIDTH)
    ak, av, bk, bv = [], [], [], []
    for i, (kind, j) in enumerate(_mixer_order(depth)):
        if kind == "a":
            q, k, kb, v, vb, gate = _pre_rows(xf, a_norm_g[j], a_w_in[j], lane_tables)
            kt_past = cache_a_k[j].transpose(0, 2, 3, 4, 1).reshape(b, WIDTH, past)
            v_past = cache_a_v[j].reshape(b, past * A_HEADS, 2 * A_DH)
            o = _attn_a_sample(q.reshape(shp), kb.reshape(shp), vb.reshape(shp), kt_past, v_past,
                               gate.reshape(shp), a_lam[j], a_subln_g[j].reshape(1, 2 * A_DH), _lam_init(i))
            ak.append(k.reshape(b, t, A_HEADS, 2, A_DH))
            av.append(v.reshape(b, t, A_HEADS, 2 * A_DH))
            w_out = a_w_out[j]
        else:
            q, k, kb, v, vb, gate = _pre_rows(xf, b_norm_g[j], b_w_in[j], None)
            kt_past = cache_b_k[j].transpose(0, 2, 3, 1).reshape(b, WIDTH, past)
            vt_past = cache_b_v[j].transpose(0, 2, 3, 1).reshape(b, WIDTH, past)
            o = _attn_b_sample(q.reshape(shp), kb.reshape(shp), vb.reshape(shp), kt_past, vt_past,
                               gate.reshape(shp), tri)
            bk.append(k.reshape(b, t, B_HEADS, B_DH))
            bv.append(v.reshape(b, t, B_HEADS, B_DH))
            w_out = b_w_out[j]
        xf = _post(o.reshape(n, WIDTH), xf, p[i].reshape(n, PLE_DIM), w_out,
                   ple_w_gate[i], ple_w_proj[i], final_norm_g if i == depth - 1 else None)
    return (xf.reshape(b, t, D_MODEL), jnp.stack(ak), jnp.stack(av), jnp.stack(bk), jnp.stack(bv))


def kernel(x_prompt, x_sample, cache_a_k, cache_a_v, cache_b_k, cache_b_v, p_prompt, p_sample,
           a_norm_g, a_w_in, a_lam_q1, a_lam_k1, a_lam_q2, a_lam_k2, a_subln_g, a_w_out,
           b_norm_g, b_w_in, b_w_out, ple_w_proj, ple_w_gate, final_norm_g):
    past = cache_a_k.shape[2]
    a_lam = jnp.stack([a_lam_q1, a_lam_k1, a_lam_q2, a_lam_k2], axis=1)
    weights = (a_norm_g, a_w_in.astype(BF16), a_lam, a_subln_g, a_w_out.astype(BF16),
               b_norm_g, b_w_in.astype(BF16), b_w_out.astype(BF16),
               ple_w_proj.astype(BF16), ple_w_gate.astype(BF16), final_norm_g)
    pos_p = jnp.arange(x_prompt.shape[1], dtype=jnp.int32)
    pos_s = past + jnp.arange(x_sample.shape[1], dtype=jnp.int32)
    y_p, ak_p, av_p, bk_p, bv_p = _trunk_prompt(x_prompt, p_prompt, pos_p, weights)
    y_s, ak_s, av_s, bk_s, bv_s = _trunk_sample(x_sample, p_sample, pos_s, weights,
                                                (cache_a_k, cache_a_v, cache_b_k, cache_b_v))
    return (y_p, y_s, ak_p, av_p, bk_p, bv_p, ak_s, av_s, bk_s, bv_s)
```

```python
import functools
import math

import jax
import jax.numpy as jnp
from jax import lax
from jax.experimental import pallas as pl
from jax.experimental.pallas import tpu as pltpu

F32 = jnp.float32
BF16 = jnp.bfloat16

D_MODEL = 1024
CHUNK = 64
PLE_DIM = 256
ROPE_THETA = 500000.0
EPS = 1e-6
NEG_INF = -1e30
A_HEADS = 8
A_DH = 64
A_ROT = A_DH // 4
B_HEADS = 16
B_DH = 64
WIDTH = 1024
LANES = 128
SUBLANES = 8
HALF = 64
Q_SCALE = A_DH ** -0.5 * math.log2(math.e)

PRE_ROWS = 256
KV_TILE = 256
TRI = 256
A_BQ, A_BK = 2048, 512
B_BQ, B_BK = 2048, 512
A_SLAB, B_SLAB = 512, 512
SAMPLE_CHUNK = 2048
VMEM_LIMIT = 48 * 1024 * 1024

_NT = (((1,), (1,)), ((), ()))


def _params(n_axes, vmem_limit=VMEM_LIMIT):
    return pltpu.CompilerParams(
        dimension_semantics=("arbitrary",) * n_axes, vmem_limit_bytes=vmem_limit)


def _sigmoid(x):
    return 1.0 / (1.0 + jnp.exp(-x))


def _mod_pow2(x, n):
    assert n & (n - 1) == 0
    return x & (n - 1)


def _chunk_of(pos):
    assert CHUNK & (CHUNK - 1) == 0
    return pos >> (CHUNK.bit_length() - 1)


def _rope_lanes(sl, c_ref, s1_ref, s2_ref):
    return (sl * c_ref[...] + pltpu.roll(sl, LANES - A_ROT // 2, 1) * s1_ref[...]
            + pltpu.roll(sl, A_ROT // 2, 1) * s2_ref[...])


def _normed(x_ref, g_ref):
    x = x_ref[...]
    ms = jnp.mean(x * x, axis=-1, keepdims=True)
    return ((x * lax.rsqrt(ms + EPS)) * g_ref[...]).astype(BF16)


def _pre_rows_body(*refs, rope):
    if rope:
        x_ref, g_ref, wq_ref, wk_ref, wv_ref, wg_ref, c_ref, s1_ref, s2_ref = refs[:9]
        outs = refs[9:]
    else:
        x_ref, g_ref, wq_ref, wk_ref, wv_ref, wg_ref = refs[:6]
        outs = refs[6:]
    q_ref, k_ref, kb_ref, v_ref, vb_ref, gate_ref = outs
    hb = _normed(x_ref, g_ref)

    def rot(r, j):
        sl = r[:, j * LANES:(j + 1) * LANES]
        return _rope_lanes(sl, c_ref, s1_ref, s2_ref) if rope else sl

    q = jnp.dot(hb, wq_ref[...], preferred_element_type=F32)
    for j in range(WIDTH // LANES):
        q_ref[:, j * LANES:(j + 1) * LANES] = (rot(q, j) * Q_SCALE).astype(BF16)
    k = jnp.dot(hb, wk_ref[...], preferred_element_type=F32)
    for j in range(WIDTH // LANES):
        kr = rot(k, j)
        k_ref[:, j * LANES:(j + 1) * LANES] = kr
        kb_ref[:, j * LANES:(j + 1) * LANES] = kr.astype(BF16)
    v = jnp.dot(hb, wv_ref[...], preferred_element_type=F32)
    v_ref[...] = v
    vb_ref[...] = v.astype(BF16)
    gate_ref[...] = jnp.dot(hb, wg_ref[...], preferred_element_type=F32)


def _pre_cols_body(*refs, rope, v_cols):
    if rope:
        x_ref, g_ref, wq_ref, wkt_ref, wv_ref, wg_ref, c_ref, s1_ref, s2_ref, ct_ref, st_ref = refs[:11]
        outs = refs[11:]
    else:
        x_ref, g_ref, wq_ref, wkt_ref, wv_ref, wg_ref = refs[:6]
        outs = refs[6:]
    q_ref, kt_ref, ktb_ref, v_ref, vb_ref, gate_ref = outs
    hb = _normed(x_ref, g_ref)

    q = jnp.dot(hb, wq_ref[...], preferred_element_type=F32)
    for j in range(WIDTH // LANES):
        sl = q[:, j * LANES:(j + 1) * LANES]
        if rope:
            sl = _rope_lanes(sl, c_ref, s1_ref, s2_ref)
        q_ref[:, j * LANES:(j + 1) * LANES] = (sl * Q_SCALE).astype(BF16)

    kt = lax.dot_general(wkt_ref[...], hb, _NT, preferred_element_type=F32)
    if rope:
        half = A_ROT // 2
        assert half == SUBLANES
        cos, sin = ct_ref[...], st_ref[...]
        for grp in range(WIDTH // A_DH):
            r0 = grp * A_DH
            x0, x1 = kt[r0:r0 + half], kt[r0 + half:r0 + 2 * half]
            y0, y1 = x0 * cos - x1 * sin, x1 * cos + x0 * sin
            kt_ref[r0:r0 + half, :] = y0
            kt_ref[r0 + half:r0 + 2 * half, :] = y1
            kt_ref[r0 + 2 * half:r0 + A_DH, :] = kt[r0 + 2 * half:r0 + A_DH]
            ktb_ref[r0:r0 + A_DH, :] = jnp.concatenate(
                [y0, y1, kt[r0 + 2 * half:r0 + A_DH]], axis=0).astype(BF16)
    else:
        kt_ref[...] = kt
        ktb_ref[...] = kt.astype(BF16)

    if v_cols:
        vt = lax.dot_general(wv_ref[...], hb, _NT, preferred_element_type=F32)
        v_ref[...] = vt
        vb_ref[...] = vt.astype(BF16)
    else:
        v = jnp.dot(hb, wv_ref[...], preferred_element_type=F32)
        v_ref[...] = v
        vb_ref[...] = v.astype(BF16)
    gate_ref[...] = jnp.dot(hb, wg_ref[...], preferred_element_type=F32)


def _split_w(w_in, t_k, t_v):
    parts = [w_in[:, i * WIDTH:(i + 1) * WIDTH] for i in range(4)]
    if t_k:
        parts[1] = parts[1].T
    if t_v:
        parts[2] = parts[2].T
    return parts


def _pre_rows(x, norm_g, w_in, tables):
    n = x.shape[0]
    rows = PRE_ROWS
    rope = tables is not None
    row_spec = lambda width: pl.BlockSpec((rows, width), lambda i: (i, 0))
    full = lambda shape: pl.BlockSpec(shape, lambda i: (0, 0))
    in_specs = [row_spec(D_MODEL), full((1, D_MODEL))] + [full((D_MODEL, WIDTH))] * 4
    args = [x, norm_g.reshape(1, D_MODEL)] + _split_w(w_in, False, False)
    if rope:
        in_specs += [full((rows, LANES))] * 3
        args += list(tables)
    out_shape = [jax.ShapeDtypeStruct((n, WIDTH), dt) for dt in (BF16, F32, BF16, F32, BF16, F32)]
    return pl.pallas_call(
        functools.partial(_pre_rows_body, rope=rope),
        grid=(n // rows,),
        in_specs=in_specs,
        out_specs=[row_spec(WIDTH)] * 6,
        out_shape=out_shape,
        compiler_params=_params(1),
        name="pre_rows_rope" if rope else "pre_rows",
    )(*args)


def _pre_cols(x, b, t, norm_g, w_in, tables, v_cols):
    rows = PRE_ROWS
    assert rows == KV_TILE and t % rows == 0
    n_t = t // rows
    rope = tables is not None
    row_spec = lambda width: pl.BlockSpec((rows, width), lambda i: (i, 0))
    full = lambda shape: pl.BlockSpec(shape, lambda i: (0, 0))
    colf_spec = pl.BlockSpec((None, WIDTH, rows), lambda i: (i // n_t, 0, i % n_t))
    colb_spec = pl.BlockSpec((None, None, WIDTH, rows), lambda i: (i // n_t, i % n_t, 0, 0))
    in_specs = [row_spec(D_MODEL), full((1, D_MODEL))] + [full((D_MODEL, WIDTH))] * 4
    args = [x, norm_g.reshape(1, D_MODEL)] + _split_w(w_in, True, v_cols)
    if rope:
        lanes_tab, rows_tab = tables
        in_specs += [pl.BlockSpec((rows, LANES), lambda i: (i % n_t, 0))] * 3
        in_specs += [pl.BlockSpec((A_ROT // 2, rows), lambda i: (0, i % n_t))] * 2
        args += list(lanes_tab) + list(rows_tab)
    colf = jax.ShapeDtypeStruct((b, WIDTH, t), F32)
    colb = jax.ShapeDtypeStruct((b, n_t, WIDTH, rows), BF16)
    rowf = jax.ShapeDtypeStruct((b * t, WIDTH), F32)
    rowb = jax.ShapeDtypeStruct((b * t, WIDTH), BF16)
    out_shape = [rowb, colf, colb] + ([colf, colb] if v_cols else [rowf, rowb]) + [rowf]
    out_specs = ([row_spec(WIDTH), colf_spec, colb_spec]
                 + ([colf_spec, colb_spec] if v_cols else [row_spec(WIDTH)] * 2) + [row_spec(WIDTH)])
    return pl.pallas_call(
        functools.partial(_pre_cols_body, rope=rope, v_cols=v_cols),
        grid=(b * n_t,),
        in_specs=in_specs,
        out_specs=out_specs,
        out_shape=out_shape,
        compiler_params=_params(1),
        name="pre_cols_rope" if rope else "pre_cols",
    )(*args)


def _rope_tables(pos, rows):
    half = A_ROT // 2
    inv = ROPE_THETA ** (-jnp.arange(half, dtype=F32) * 2.0 / A_ROT)
    ang = pos.astype(F32)[:, None] * inv[None, :]
    cos, sin = jnp.cos(ang), jnp.sin(ang)
    t = pos.shape[0]
    c = jnp.concatenate([cos, cos, jnp.ones((t, HALF - A_ROT), F32)], axis=1)
    s1 = jnp.concatenate([-sin, jnp.zeros((t, HALF - half), F32)], axis=1)
    s2 = jnp.concatenate([jnp.zeros((t, half), F32), sin, jnp.zeros((t, HALF - A_ROT), F32)], axis=1)
    reps = (max(rows // t, 1), LANES // HALF)
    return tuple(jnp.tile(a, reps) for a in (c, s1, s2)), (cos.T, sin.T)


def _lo_lanes():
    return lax.broadcasted_iota(jnp.int32, (1, LANES), 1) < HALF


def _halves(q):
    lo = _lo_lanes()
    zero = jnp.zeros_like(q)
    return jnp.where(lo, q, zero), jnp.where(lo, zero, q)


def _lam(lam_ref, lam_init):
    a = jnp.sum(lam_ref[0:1, :] * lam_ref[1:2, :], axis=-1, keepdims=True)
    b = jnp.sum(lam_ref[2:3, :] * lam_ref[3:4, :], axis=-1, keepdims=True)
    return jnp.exp(a) - jnp.exp(b) + lam_init


def _diff_epilogue(o1, o2, lam_ref, subg_ref, g_ref, o_ref, lam_init):
    o = o1 - _lam(lam_ref, lam_init) * o2
    ms = jnp.mean(o * o, axis=-1, keepdims=True)
    o = (o * lax.rsqrt(ms + EPS)) * subg_ref[...] * (1.0 - lam_init)
    g = g_ref[...]
    o_ref[...] = (o * (g * _sigmoid(g))).astype(o_ref.dtype)


def _softplus2(z):
    sign_bit = jnp.uint32(1 << 31)
    neg_abs = lax.bitcast_convert_type(lax.bitcast_convert_type(z, jnp.uint32) | sign_bit, F32)
    return jnp.maximum(z, 0.0) + jnp.log2(1.0 + jnp.exp2(neg_abs))


def _stick_weights(z, valid, carry, tri):
    sp = _softplus2(z)
    if valid is not None:
        sp = jnp.where(valid, sp, 0.0)
    pieces = [None] * (z.shape[1] // TRI)
    for sb in range(len(pieces) - 1, -1, -1):
        cols = slice(sb * TRI, (sb + 1) * TRI)
        sps = sp[:, cols]
        suffix = jnp.dot(sps.astype(BF16), tri, preferred_element_type=F32)
        pieces[sb] = jnp.exp2(z[:, cols] - suffix - carry)
        carry = carry + suffix[:, :1]
    a = pieces[0] if len(pieces) == 1 else jnp.concatenate(pieces, axis=1)
    if valid is not None:
        a = jnp.where(valid, a, 0.0)
    return a, carry


def _kv_tiles(ref, kj, bk):
    per = bk // KV_TILE
    tiles = [ref[kj * per + i] for i in range(per)]
    return tiles[0] if per == 1 else jnp.concatenate(tiles, axis=1)


def _attn_a_prompt_body(lam_ref, subg_ref, q_ref, kt_ref, v_ref, g_ref, o_ref,
                        qz_ref, m_ref, l_ref, acc_ref, *, bq, bk, diag_slab, lam_init):
    qi = pl.program_id(2)
    q1, q2 = _halves(q_ref[...])
    qz_ref[0] = q1
    qz_ref[1] = q2
    m_ref[...] = jnp.full(m_ref.shape, NEG_INF, F32)
    l_ref[...] = jnp.zeros(l_ref.shape, F32)
    acc_ref[...] = jnp.zeros(acc_ref.shape, F32)

    def step(kj, diag):
        kt = _kv_tiles(kt_ref, kj, bk)
        v = v_ref[pl.ds(pl.multiple_of(kj * bk, bk), bk), :]
        if diag is not None:
            v = jnp.concatenate([v, jnp.ones_like(v)], axis=1)
        slab = bq if diag is None else diag_slab
        for r0 in range(0, bq, slab):
            if diag is None or _chunk_of((diag + 1) * bk - 1) <= _chunk_of(r0):
                vis = None
            elif _chunk_of(diag * bk) > _chunk_of(r0 + slab - 1):
                continue
            else:
                qpos = r0 + lax.broadcasted_iota(jnp.int32, (slab, bk), 0)
                kpos = diag * bk + lax.broadcasted_iota(jnp.int32, (slab, bk), 1)
                vis = _chunk_of(kpos) <= _chunk_of(qpos)
            rows = slice(r0, r0 + slab)
            for c in range(2):
                s = jnp.dot(qz_ref[c, rows, :], kt, preferred_element_type=F32)
                if vis is not None:
                    s = jnp.where(vis, s, NEG_INF)
                m_prev = m_ref[c, rows, :]
                m_new = jnp.maximum(m_prev, jnp.max(s, axis=-1, keepdims=True))
                alpha = jnp.exp2(m_prev - m_new)
                p = jnp.exp2(s - jnp.tile(m_new, (1, bk // LANES)))
                pv = jnp.dot(p.astype(BF16), v, preferred_element_type=F32)
                psum = jnp.sum(p, axis=-1, keepdims=True) if diag is None else pv[:, LANES:]
                l_ref[c, rows, :] = alpha * l_ref[c, rows, :] + psum
                acc_ref[c, rows, :] = alpha * acc_ref[c, rows, :] + pv[:, :LANES]
                m_ref[c, rows, :] = m_new

    n_full = qi * (bq // bk)

    def body(kj, c):
        step(kj, None)
        return c

    lax.fori_loop(0, n_full, body, 0)
    for r in range(bq // bk):
        step(n_full + r, r)

    _diff_epilogue(acc_ref[0] / l_ref[0], acc_ref[1] / l_ref[1],
                   lam_ref, subg_ref, g_ref, o_ref, lam_init)


def _attn_a_prompt(q, ktb, vb, gate, lam4, subg, lam_init):
    b, t, _ = q.shape
    bq, bk = A_BQ, A_BK
    assert bq % bk == 0 and bk % KV_TILE == 0 and t % bq == 0 and bk % CHUNK == 0
    q_spec = pl.BlockSpec((None, bq, LANES), lambda bi, h, qi: (bi, qi, h))
    kt_spec = pl.BlockSpec((None, t // KV_TILE, LANES, KV_TILE), lambda bi, h, qi: (bi, 0, h, 0))
    v_spec = pl.BlockSpec((None, t, LANES), lambda bi, h, qi: (bi, 0, h))
    full = lambda shape: pl.BlockSpec(shape, lambda bi, h, qi: (0,) * len(shape))
    return pl.pallas_call(
        functools.partial(_attn_a_prompt_body, bq=bq, bk=bk, diag_slab=A_SLAB, lam_init=lam_init),
        grid=(b, A_HEADS, t // bq),
        in_specs=[full(lam4.shape), full(subg.shape), q_spec, kt_spec, v_spec, q_spec],
        out_specs=q_spec,
        out_shape=jax.ShapeDtypeStruct((b, t, WIDTH), BF16),
        scratch_shapes=[pltpu.VMEM((2, bq, LANES), BF16),
                        pltpu.VMEM((2, bq, LANES), F32),
                        pltpu.VMEM((2, bq, LANES), F32),
                        pltpu.VMEM((2, bq, LANES), F32)],
        compiler_params=_params(3),
        name="attn_a_prompt",
    )(lam4, subg, q, ktb, vb, gate)


def _attn_b_prompt_body(tri_ref, q_ref, kt_ref, vt_ref, g_ref, o_ref,
                        qz_ref, carry_ref, acc_ref, *, bq, bk, diag_slab):
    qi = pl.program_id(2)
    q1, q2 = _halves(q_ref[...])
    qz_ref[0] = q1
    qz_ref[1] = q2
    carry_ref[...] = jnp.zeros(carry_ref.shape, F32)
    acc_ref[...] = jnp.zeros(acc_ref.shape, F32)

    def step(kj, diag):
        kt = _kv_tiles(kt_ref, kj, bk)
        vt = _kv_tiles(vt_ref, kj, bk)
        slab = bq if diag is None else diag_slab
        for r0 in range(0, bq, slab):
            if diag is None or (diag + 1) * bk <= r0:
                valid = None
            elif diag * bk >= r0 + slab - 1:
                continue
            else:
                qpos = r0 + lax.broadcasted_iota(jnp.int32, (slab, bk), 0)
                kpos = diag * bk + lax.broadcasted_iota(jnp.int32, (slab, bk), 1)
                valid = kpos < qpos
            rows = slice(r0, r0 + slab)
            zs = [jnp.dot(qz_ref[c, rows, :], kt, preferred_element_type=F32) for c in range(2)]
            sps = [_softplus2(z) for z in zs]
            if valid is not None:
                sps = [jnp.where(valid, sp, 0.0) for sp in sps]
            carries = [carry_ref[c, rows, :] for c in range(2)]
            pieces = [[None] * (bk // TRI) for _ in range(2)]
            for sb in range(bk // TRI - 1, -1, -1):
                cols = slice(sb * TRI, (sb + 1) * TRI)
                sfx = [jnp.dot(sp[:, cols].astype(BF16), tri_ref[...], preferred_element_type=F32) for sp in sps]
                for c in range(2):
                    pieces[c][sb] = jnp.exp2(zs[c][:, cols] - sfx[c] - carries[c])
                    carries[c] = carries[c] + sfx[c][:, :1]
            for c in range(2):
                a = jnp.concatenate(pieces[c], axis=1)
                if valid is not None:
                    a = jnp.where(valid, a, 0.0)
                acc_ref[c, rows, :] += lax.dot_general(a.astype(BF16), vt, _NT, preferred_element_type=F32)
                carry_ref[c, rows, :] = carries[c]

    n_full = qi * (bq // bk)
    for r in range(bq // bk - 1, -1, -1):
        step(n_full + r, r)

    def body(t, c):
        step(n_full - 1 - t, None)
        return c

    lax.fori_loop(0, n_full, body, 0)

    g = g_ref[...]
    o = jnp.where(_lo_lanes(), acc_ref[0], acc_ref[1])
    o_ref[...] = (o * (g * _sigmoid(g))).astype(o_ref.dtype)


def _attn_b_prompt(q, ktb, vtb, gate, tri):
    b, t, _ = q.shape
    bq, bk = B_BQ, B_BK
    assert bq % bk == 0 and bk % KV_TILE == 0 and t % bq == 0 and bk % TRI == 0
    q_spec = pl.BlockSpec((None, bq, LANES), lambda bi, h, qi: (bi, qi, h))
    kt_spec = pl.BlockSpec((None, t // KV_TILE, LANES, KV_TILE), lambda bi, h, qi: (bi, 0, h, 0))
    return pl.pallas_call(
        functools.partial(_attn_b_prompt_body, bq=bq, bk=bk, diag_slab=B_SLAB),
        grid=(b, WIDTH // LANES, t // bq),
        in_specs=[pl.BlockSpec((TRI, TRI), lambda bi, h, qi: (0, 0)),
                  q_spec, kt_spec, kt_spec, q_spec],
        out_specs=q_spec,
        out_shape=jax.ShapeDtypeStruct((b, t, WIDTH), BF16),
        scratch_shapes=[pltpu.VMEM((2, bq, LANES), BF16),
                        pltpu.VMEM((2, bq, 1), F32),
                        pltpu.VMEM((2, bq, LANES), F32)],
        compiler_params=_params(3),
        name="attn_b_prompt",
    )(tri, q, ktb, vtb, gate)


def _pad_rows(x, rows):
    return jnp.concatenate([x, jnp.zeros((rows - x.shape[0], x.shape[1]), x.dtype)], axis=0)


def _attn_a_sample_body(lam_ref, subg_ref, q_ref, kn_ref, vn_ref, ktp_ref, vp_ref, g_ref, o_ref,
                        qq_ref, m_ref, l_ref, acc_ref, *, t_new, past, chunk, lam_init):
    pc = pl.program_id(1)

    @pl.when(pc == 0)
    def _():
        for h in range(A_HEADS):
            qq_ref[h] = jnp.concatenate(_halves(q_ref[:, h * LANES:(h + 1) * LANES]), axis=0)
        m_ref[...] = jnp.full(m_ref.shape, NEG_INF, F32)
        l_ref[...] = jnp.zeros(l_ref.shape, F32)
        acc_ref[...] = jnp.zeros(acc_ref.shape, F32)

    def update(h, s, v):
        m_prev = m_ref[h]
        m_new = jnp.maximum(m_prev, jnp.max(s, axis=-1, keepdims=True))
        alpha = jnp.exp2(m_prev - m_new)
        p = jnp.exp2(s - jnp.tile(m_new, (1, s.shape[1] // LANES)))
        l_ref[h] = alpha * l_ref[h] + jnp.sum(p, axis=-1, keepdims=True)
        acc_ref[h] = alpha * acc_ref[h] + jnp.dot(p.astype(BF16), v, preferred_element_type=F32)
        m_ref[h] = m_new

    for h in range(A_HEADS):
        kt = ktp_ref[h * LANES:(h + 1) * LANES, :].astype(BF16)
        v = vp_ref[pl.ds(h, chunk, stride=A_HEADS), :].astype(BF16)
        update(h, jnp.dot(qq_ref[h], kt, preferred_element_type=F32), v)

    @pl.when(pc == pl.num_programs(1) - 1)
    def _():
        row = lax.broadcasted_iota(jnp.int32, (2 * t_new, LANES), 0)
        col = lax.broadcasted_iota(jnp.int32, (2 * t_new, LANES), 1)
        vis = jnp.logical_and(col < t_new,
                              _chunk_of(past + col) <= _chunk_of(past + _mod_pow2(row, t_new)))
        for h in range(A_HEADS):
            cols = slice(h * LANES, (h + 1) * LANES)
            kn = _pad_rows(kn_ref[:, cols], LANES)
            vn = _pad_rows(vn_ref[:, cols], LANES)
            s_n = lax.dot_general(qq_ref[h], kn, _NT, preferred_element_type=F32)
            update(h, jnp.where(vis, s_n, NEG_INF), vn)
            o = acc_ref[h] / l_ref[h]
            _diff_epilogue(o[:t_new], o[t_new:], lam_ref, subg_ref, g_ref.at[:, cols], o_ref.at[:, cols],
                           lam_init)


def _attn_a_sample(q, kb, vb, kt_past, v_past, gate, lam4, subg, lam_init):
    b, t_new, _ = q.shape
    past = kt_past.shape[2]
    chunk = SAMPLE_CHUNK
    assert v_past.shape[1] == past * A_HEADS and past % chunk == 0
    new_spec = pl.BlockSpec((None, t_new, WIDTH), lambda bi, pc: (bi, 0, 0))
    ktp_spec = pl.BlockSpec((None, WIDTH, chunk), lambda bi, pc: (bi, 0, pc))
    vp_spec = pl.BlockSpec((None, chunk * A_HEADS, LANES), lambda bi, pc: (bi, pc, 0))
    full = lambda shape: pl.BlockSpec(shape, lambda bi, pc: (0,) * len(shape))
    state = pltpu.VMEM((A_HEADS, 2 * t_new, LANES), F32)
    return pl.pallas_call(
        functools.partial(_attn_a_sample_body, t_new=t_new, past=past, chunk=chunk, lam_init=lam_init),
        grid=(b, past // chunk),
        in_specs=[full(lam4.shape), full(subg.shape), new_spec, new_spec, new_spec,
                  ktp_spec, vp_spec, new_spec],
        out_specs=new_spec,
        out_shape=jax.ShapeDtypeStruct((b, t_new, WIDTH), BF16),
        scratch_shapes=[pltpu.VMEM((A_HEADS, 2 * t_new, LANES), BF16), state, state, state],
        compiler_params=_params(2),
        name="attn_a_sample",
    )(lam4, subg, q, kb, vb, kt_past, v_past, gate)


def _attn_b_sample_body(tri_ref, q_ref, kn_ref, vn_ref, ktp_ref, vtp_ref, g_ref, o_ref,
                        qq_ref, carry_ref, acc_ref, *, t_new, chunk):
    pc = pl.program_id(1)
    rows = 2 * t_new
    n_blk = chunk // TRI
    n_pairs = WIDTH // LANES
    tri = tri_ref[...]

    @pl.when(pc == 0)
    def _():
        row = lax.broadcasted_iota(jnp.int32, (rows, TRI), 0)
        col = lax.broadcasted_iota(jnp.int32, (rows, TRI), 1)
        valid = col < _mod_pow2(row, t_new)
        for hp in range(n_pairs):
            cols = slice(hp * LANES, (hp + 1) * LANES)
            qq = jnp.concatenate(_halves(q_ref[:, cols]), axis=0)
            qq_ref[hp] = qq
            kn = _pad_rows(kn_ref[:, cols], TRI)
            vn = _pad_rows(vn_ref[:, cols], TRI)
            z_n = lax.dot_general(qq, kn, _NT, preferred_element_type=F32)
            a_n, carry = _stick_weights(z_n, valid, jnp.zeros((rows, 1), F32), tri)
            acc_ref[hp] = jnp.dot(a_n.astype(BF16), vn, preferred_element_type=F32)
            carry_ref[hp] = carry

    for hp in range(n_pairs):
        feat = slice(hp * LANES, (hp + 1) * LANES)
        z_all = jnp.dot(qq_ref[hp], ktp_ref[feat, :].astype(BF16), preferred_element_type=F32)
        z = jnp.concatenate([z_all[:, c * TRI:(c + 1) * TRI] for c in range(n_blk)], axis=0)
        suffix = jnp.dot(_softplus2(z).astype(BF16), tri, preferred_element_type=F32)
        running = carry_ref[hp]
        carries = [None] * n_blk
        for c in range(n_blk - 1, -1, -1):
            carries[c] = running
            running = running + suffix[c * rows:(c + 1) * rows, :1]
        carry_ref[hp] = running
        a = jnp.exp2(z - suffix - jnp.concatenate(carries, axis=0)).astype(BF16)
        vtp = vtp_ref[feat, :].astype(BF16)
        acc = acc_ref[hp]
        for c in range(n_blk):
            acc = acc + lax.dot_general(a[c * rows:(c + 1) * rows], vtp[:, c * TRI:(c + 1) * TRI], _NT,
                                        preferred_element_type=F32)
        acc_ref[hp] = acc

    @pl.when(pc == pl.num_programs(1) - 1)
    def _():
        for hp in range(n_pairs):
            cols = slice(hp * LANES, (hp + 1) * LANES)
            acc = acc_ref[hp]
            g = g_ref[:, cols]
            o = jnp.where(_lo_lanes(), acc[:t_new], acc[t_new:])
            o_ref[:, cols] = (o * (g * _sigmoid(g))).astype(o_ref.dtype)


def _attn_b_sample(q, kb, vb, kt_past, vt_past, gate, tri):
    b, t_new, _ = q.shape
    past = kt_past.shape[2]
    chunk = SAMPLE_CHUNK
    assert past % chunk == 0 and chunk % TRI == 0 and t_new <= TRI
    n_pc = past // chunk
    n_pairs = WIDTH // LANES
    new_spec = pl.BlockSpec((None, t_new, WIDTH), lambda bi, pc: (bi, 0, 0))
    past_spec = pl.BlockSpec((None, WIDTH, chunk), lambda bi, pc: (bi, 0, n_pc - 1 - pc))
    return pl.pallas_call(
        functools.partial(_attn_b_sample_body, t_new=t_new, chunk=chunk),
        grid=(b, n_pc),
        in_specs=[pl.BlockSpec((TRI, TRI), lambda bi, pc: (0, 0)),
                  new_spec, new_spec, new_spec, past_spec, past_spec, new_spec],
        out_specs=new_spec,
        out_shape=jax.ShapeDtypeStruct((b, t_new, WIDTH), BF16),
        scratch_shapes=[pltpu.VMEM((n_pairs, 2 * t_new, LANES), BF16),
                        pltpu.VMEM((n_pairs, 2 * t_new, 1), F32),
                        pltpu.VMEM((n_pairs, 2 * t_new, LANES), F32)],
        compiler_params=_params(2),
        name="attn_b_sample",
    )(tri, q, kb, vb, kt_past, vt_past, gate)


def _post_body(*refs, final):
    if final:
        o_ref, x_ref, p_ref, wo_ref, wg_ref, wp_ref, fg_ref, out_ref = refs
    else:
        o_ref, x_ref, p_ref, wo_ref, wg_ref, wp_ref, out_ref = refs
    x = x_ref[...] + jnp.dot(o_ref[...], wo_ref[...], preferred_element_type=F32)
    gate = _sigmoid(jnp.dot(x.astype(BF16), wg_ref[...], preferred_element_type=F32))
    x = x + gate * jnp.dot(p_ref[...].astype(BF16), wp_ref[...], preferred_element_type=F32)
    if final:
        ms = jnp.mean(x * x, axis=-1, keepdims=True)
        x = (x * lax.rsqrt(ms + EPS)) * fg_ref[...]
    out_ref[...] = x


def _post(o, x, p, layer, w_out, w_gate, w_proj, final_g):
    n = x.shape[0]
    rows = PRE_ROWS
    final = final_g is not None
    row_spec = lambda width: pl.BlockSpec((rows, width), lambda i: (i, 0))
    full = lambda shape: pl.BlockSpec(shape, lambda i: (0, 0))
    in_specs = [row_spec(WIDTH), row_spec(D_MODEL),
                pl.BlockSpec((None, rows, PLE_DIM), lambda i: (layer, i, 0)),
                full(w_out.shape), full(w_gate.shape), full(w_proj.shape)]
    args = [o, x, p, w_out, w_gate, w_proj]
    if final:
        in_specs.append(full((1, D_MODEL)))
        args.append(final_g.reshape(1, D_MODEL))
    return pl.pallas_call(
        functools.partial(_post_body, final=final),
        grid=(n // rows,),
        in_specs=in_specs,
        out_specs=row_spec(D_MODEL),
        out_shape=jax.ShapeDtypeStruct((n, D_MODEL), F32),
        compiler_params=_params(1),
        name="post_final" if final else "post",
    )(*args)


def _mixer_order(depth):
    return [("a", i // 2) if i % 2 == 0 else ("b", i // 2) for i in range(depth)]


def _lam_init(i):
    return 0.8 - 0.6 * math.exp(-0.3 * i)


def _trunk_prompt(x, p, pos, weights):
    (a_norm_g, a_w_in, a_lam, a_subln_g, a_w_out, b_norm_g, b_w_in, b_w_out,
     ple_w_proj, ple_w_gate, final_norm_g) = weights
    b, t, _ = x.shape
    n = b * t
    depth = p.shape[0]
    xf = x.reshape(n, D_MODEL)
    tables = _rope_tables(pos, PRE_ROWS)
    tri = jnp.tri(TRI, dtype=BF16)
    shp = (b, t, WIDTH)
    ak, av, bk, bv = [], [], [], []
    for i, (kind, j) in enumerate(_mixer_order(depth)):
        if kind == "a":
            q, kt, ktb, v, vb, gate = _pre_cols(xf, b, t, a_norm_g[j], a_w_in[j], tables, False)
            o = _attn_a_prompt(q.reshape(shp), ktb, vb.reshape(shp), gate.reshape(shp),
                               a_lam[j], a_subln_g[j].reshape(1, 2 * A_DH), _lam_init(i))
            ak.append(kt.reshape(b, A_HEADS, 2, A_DH, t).transpose(0, 4, 1, 2, 3))
            av.append(v.reshape(b, t, A_HEADS, 2 * A_DH))
            w_out = a_w_out[j]
        else:
            q, kt, ktb, vt, vtb, gate = _pre_cols(xf, b, t, b_norm_g[j], b_w_in[j], None, True)
            o = _attn_b_prompt(q.reshape(shp), ktb, vtb, gate.reshape(shp), tri)
            bk.append(kt.reshape(b, B_HEADS, B_DH, t).transpose(0, 3, 1, 2))
            bv.append(vt.reshape(b, B_HEADS, B_DH, t).transpose(0, 3, 1, 2))
            w_out = b_w_out[j]
        xf = _post(o.reshape(n, WIDTH), xf, p.reshape(depth, n, PLE_DIM), i, w_out,
                   ple_w_gate[i], ple_w_proj[i], final_norm_g if i == depth - 1 else None)
    return (xf.reshape(b, t, D_MODEL), jnp.stack(ak), jnp.stack(av), jnp.stack(bk), jnp.stack(bv))


def _trunk_sample(x, p, pos, weights, caches):
    (a_norm_g, a_w_in, a_lam, a_subln_g, a_w_out, b_norm_g, b_w_in, b_w_out,
     ple_w_proj, ple_w_gate, final_norm_g) = weights
    cache_a_k, cache_a_v, cache_b_k, cache_b_v = caches
    b, t, _ = x.shape
    n = b * t
    depth = p.shape[0]
    past = cache_a_k.shape[2]
    xf = x.reshape(n, D_MODEL)
    lane_tables, _ = _rope_tables(pos, PRE_ROWS)
    tri = jnp.tri(TRI, dtype=BF16)
    shp = (b, t, WIDTH)
    ak, av, bk, bv = [], [], [], []
    for i, (kind, j) in enumerate(_mixer_order(depth)):
        if kind == "a":
            q, k, kb, v, vb, gate = _pre_rows(xf, a_norm_g[j], a_w_in[j], lane_tables)
            kt_past = cache_a_k[j].transpose(0, 2, 3, 4, 1).reshape(b, WIDTH, past)
            v_past = cache_a_v[j].reshape(b, past * A_HEADS, 2 * A_DH)
            o = _attn_a_sample(q.reshape(shp), kb.reshape(shp), vb.reshape(shp), kt_past, v_past,
                               gate.reshape(shp), a_lam[j], a_subln_g[j].reshape(1, 2 * A_DH), _lam_init(i))
            ak.append(k.reshape(b, t, A_HEADS, 2, A_DH))
            av.append(v.reshape(b, t, A_HEADS, 2 * A_DH))
            w_out = a_w_out[j]
        else:
            q, k, kb, v, vb, gate = _pre_rows(xf, b_norm_g[j], b_w_in[j], None)
            kt_past = cache_b_k[j].transpose(0, 2, 3, 1).reshape(b, WIDTH, past)
            vt_past = cache_b_v[j].transpose(0, 2, 3, 1).reshape(b, WIDTH, past)
            o = _attn_b_sample(q.reshape(shp), kb.reshape(shp), vb.reshape(shp), kt_past, vt_past,
                               gate.reshape(shp), tri)
            bk.append(k.reshape(b, t, B_HEADS, B_DH))
            bv.append(v.reshape(b, t, B_HEADS, B_DH))
            w_out = b_w_out[j]
        xf = _post(o.reshape(n, WIDTH), xf, p.reshape(depth, n, PLE_DIM), i, w_out,
                   ple_w_gate[i], ple_w_proj[i], final_norm_g if i == depth - 1 else None)
    return (xf.reshape(b, t, D_MODEL), jnp.stack(ak), jnp.stack(av), jnp.stack(bk), jnp.stack(bv))


def kernel(x_prompt, x_sample, cache_a_k, cache_a_v, cache_b_k, cache_b_v, p_prompt, p_sample,
           a_norm_g, a_w_in, a_lam_q1, a_lam_k1, a_lam_q2, a_lam_k2, a_subln_g, a_w_out,
           b_norm_g, b_w_in, b_w_out, ple_w_proj, ple_w_gate, final_norm_g):
    past = cache_a_k.shape[2]
    a_lam = jnp.stack([a_lam_q1, a_lam_k1, a_lam_q2, a_lam_k2], axis=1)
    weights = (a_norm_g, a_w_in.astype(BF16), a_lam, a_subln_g, a_w_out.astype(BF16),
               b_norm_g, b_w_in.astype(BF16), b_w_out.astype(BF16),
               ple_w_proj.astype(BF16), ple_w_gate.astype(BF16), final_norm_g)
    pos_p = jnp.arange(x_prompt.shape[1], dtype=jnp.int32)
    pos_s = past + jnp.arange(x_sample.shape[1], dtype=jnp.int32)
    y_p, ak_p, av_p, bk_p, bv_p = _trunk_prompt(x_prompt, p_prompt, pos_p, weights)
    y_s, ak_s, av_s, bk_s, bv_s = _trunk_sample(x_sample, p_sample, pos_s, weights,
                                                (cache_a_k, cache_a_v, cache_b_k, cache_b_v))
    return (y_p, y_s, ak_p, av_p, bk_p, bv_p, ak_s, av_s, bk_s, bv_s)
```

```python
import functools
import math

import jax
import jax.numpy as jnp
from jax import lax
from jax.experimental import pallas as pl
from jax.experimental.pallas import tpu as pltpu

F32 = jnp.float32
BF16 = jnp.bfloat16

D_MODEL = 1024
CHUNK = 64
PLE_DIM = 256
ROPE_THETA = 500000.0
EPS = 1e-6
NEG_INF = -1e30
A_HEADS = 8
A_DH = 64
A_ROT = A_DH // 4
B_HEADS = 16
B_DH = 64
WIDTH = 1024
LANES = 128
SUBLANES = 8
HALF = 64
Q_SCALE = A_DH ** -0.5 * math.log2(math.e)

PRE_ROWS = 256
KV_TILE = 256
TRI = 256
A_BQ, A_BK = 2048, 512
B_BQ, B_BK = 256, 256
A_SLAB, B_SLAB = 512, 256
STICK_EXIT = 151.0
SAMPLE_CHUNK = 2048
VMEM_LIMIT = 48 * 1024 * 1024

_NT = (((1,), (1,)), ((), ()))


def _params(n_axes, vmem_limit=VMEM_LIMIT):
    return pltpu.CompilerParams(
        dimension_semantics=("arbitrary",) * n_axes, vmem_limit_bytes=vmem_limit)


def _sigmoid(x):
    return 1.0 / (1.0 + jnp.exp(-x))


def _mod_pow2(x, n):
    assert n & (n - 1) == 0
    return x & (n - 1)


def _chunk_of(pos):
    assert CHUNK & (CHUNK - 1) == 0
    return pos >> (CHUNK.bit_length() - 1)


def _rope_lanes(sl, c_ref, s1_ref, s2_ref):
    return (sl * c_ref[...] + pltpu.roll(sl, LANES - A_ROT // 2, 1) * s1_ref[...]
            + pltpu.roll(sl, A_ROT // 2, 1) * s2_ref[...])


def _normed(x_ref, g_ref):
    x = x_ref[...]
    ms = jnp.mean(x * x, axis=-1, keepdims=True)
    return ((x * lax.rsqrt(ms + EPS)) * g_ref[...]).astype(BF16)


def _pre_rows_body(*refs, rope):
    if rope:
        x_ref, g_ref, wq_ref, wk_ref, wv_ref, wg_ref, c_ref, s1_ref, s2_ref = refs[:9]
        outs = refs[9:]
    else:
        x_ref, g_ref, wq_ref, wk_ref, wv_ref, wg_ref = refs[:6]
        outs = refs[6:]
    q_ref, k_ref, kb_ref, v_ref, vb_ref, gate_ref = outs
    hb = _normed(x_ref, g_ref)

    def rot(r, j):
        sl = r[:, j * LANES:(j + 1) * LANES]
        return _rope_lanes(sl, c_ref, s1_ref, s2_ref) if rope else sl

    q = jnp.dot(hb, wq_ref[...], preferred_element_type=F32)
    for j in range(WIDTH // LANES):
        q_ref[:, j * LANES:(j + 1) * LANES] = (rot(q, j) * Q_SCALE).astype(BF16)
    k = jnp.dot(hb, wk_ref[...], preferred_element_type=F32)
    for j in range(WIDTH // LANES):
        kr = rot(k, j)
        k_ref[:, j * LANES:(j + 1) * LANES] = kr
        kb_ref[:, j * LANES:(j + 1) * LANES] = kr.astype(BF16)
    v = jnp.dot(hb, wv_ref[...], preferred_element_type=F32)
    v_ref[...] = v
    vb_ref[...] = v.astype(BF16)
    gate_ref[...] = jnp.dot(hb, wg_ref[...], preferred_element_type=F32)


def _pre_cols_body(*refs, rope, v_cols):
    if rope:
        x_ref, g_ref, wq_ref, wkt_ref, wv_ref, wg_ref, c_ref, s1_ref, s2_ref, ct_ref, st_ref = refs[:11]
        outs = refs[11:]
    else:
        x_ref, g_ref, wq_ref, wkt_ref, wv_ref, wg_ref = refs[:6]
        outs = refs[6:]
    q_ref, kt_ref, ktb_ref, v_ref, vb_ref, gate_ref = outs
    hb = _normed(x_ref, g_ref)

    q = jnp.dot(hb, wq_ref[...], preferred_element_type=F32)
    for j in range(WIDTH // LANES):
        sl = q[:, j * LANES:(j + 1) * LANES]
        if rope:
            sl = _rope_lanes(sl, c_ref, s1_ref, s2_ref)
        q_ref[:, j * LANES:(j + 1) * LANES] = (sl * Q_SCALE).astype(BF16)

    kt = lax.dot_general(wkt_ref[...], hb, _NT, preferred_element_type=F32)
    if rope:
        half = A_ROT // 2
        assert half == SUBLANES
        cos, sin = ct_ref[...], st_ref[...]
        for grp in range(WIDTH // A_DH):
            r0 = grp * A_DH
            x0, x1 = kt[r0:r0 + half], kt[r0 + half:r0 + 2 * half]
            y0, y1 = x0 * cos - x1 * sin, x1 * cos + x0 * sin
            kt_ref[r0:r0 + half, :] = y0
            kt_ref[r0 + half:r0 + 2 * half, :] = y1
            kt_ref[r0 + 2 * half:r0 + A_DH, :] = kt[r0 + 2 * half:r0 + A_DH]
            ktb_ref[r0:r0 + A_DH, :] = jnp.concatenate(
                [y0, y1, kt[r0 + 2 * half:r0 + A_DH]], axis=0).astype(BF16)
    else:
        kt_ref[...] = kt
        ktb_ref[...] = kt.astype(BF16)

    if v_cols:
        vt = lax.dot_general(wv_ref[...], hb, _NT, preferred_element_type=F32)
        v_ref[...] = vt
        vb_ref[...] = vt.astype(BF16)
    else:
        v = jnp.dot(hb, wv_ref[...], preferred_element_type=F32)
        v_ref[...] = v
        vb_ref[...] = v.astype(BF16)
    gate_ref[...] = jnp.dot(hb, wg_ref[...], preferred_element_type=F32)


def _split_w(w_in, t_k, t_v):
    parts = [w_in[:, i * WIDTH:(i + 1) * WIDTH] for i in range(4)]
    if t_k:
        parts[1] = parts[1].T
    if t_v:
        parts[2] = parts[2].T
    return parts


def _pre_rows(x, norm_g, w_in, tables):
    n = x.shape[0]
    rows = PRE_ROWS
    rope = tables is not None
    row_spec = lambda width: pl.BlockSpec((rows, width), lambda i: (i, 0))
    full = lambda shape: pl.BlockSpec(shape, lambda i: (0, 0))
    in_specs = [row_spec(D_MODEL), full((1, D_MODEL))] + [full((D_MODEL, WIDTH))] * 4
    args = [x, norm_g.reshape(1, D_MODEL)] + _split_w(w_in, False, False)
    if rope:
        in_specs += [full((rows, LANES))] * 3
        args += list(tables)
    out_shape = [jax.ShapeDtypeStruct((n, WIDTH), dt) for dt in (BF16, F32, BF16, F32, BF16, F32)]
    return pl.pallas_call(
        functools.partial(_pre_rows_body, rope=rope),
        grid=(n // rows,),
        in_specs=in_specs,
        out_specs=[row_spec(WIDTH)] * 6,
        out_shape=out_shape,
        compiler_params=_params(1),
        name="pre_rows_rope" if rope else "pre_rows",
    )(*args)


def _pre_cols(x, b, t, norm_g, w_in, tables, v_cols):
    rows = PRE_ROWS
    assert rows == KV_TILE and t % rows == 0
    n_t = t // rows
    rope = tables is not None
    row_spec = lambda width: pl.BlockSpec((rows, width), lambda i: (i, 0))
    full = lambda shape: pl.BlockSpec(shape, lambda i: (0, 0))
    colf_spec = pl.BlockSpec((None, WIDTH, rows), lambda i: (i // n_t, 0, i % n_t))
    colb_spec = pl.BlockSpec((None, None, WIDTH, rows), lambda i: (i // n_t, i % n_t, 0, 0))
    in_specs = [row_spec(D_MODEL), full((1, D_MODEL))] + [full((D_MODEL, WIDTH))] * 4
    args = [x, norm_g.reshape(1, D_MODEL)] + _split_w(w_in, True, v_cols)
    if rope:
        lanes_tab, rows_tab = tables
        in_specs += [pl.BlockSpec((rows, LANES), lambda i: (i % n_t, 0))] * 3
        in_specs += [pl.BlockSpec((A_ROT // 2, rows), lambda i: (0, i % n_t))] * 2
        args += list(lanes_tab) + list(rows_tab)
    colf = jax.ShapeDtypeStruct((b, WIDTH, t), F32)
    colb = jax.ShapeDtypeStruct((b, n_t, WIDTH, rows), BF16)
    rowf = jax.ShapeDtypeStruct((b * t, WIDTH), F32)
    rowb = jax.ShapeDtypeStruct((b * t, WIDTH), BF16)
    out_shape = [rowb, colf, colb] + ([colf, colb] if v_cols else [rowf, rowb]) + [rowf]
    out_specs = ([row_spec(WIDTH), colf_spec, colb_spec]
                 + ([colf_spec, colb_spec] if v_cols else [row_spec(WIDTH)] * 2) + [row_spec(WIDTH)])
    return pl.pallas_call(
        functools.partial(_pre_cols_body, rope=rope, v_cols=v_cols),
        grid=(b * n_t,),
        in_specs=in_specs,
        out_specs=out_specs,
        out_shape=out_shape,
        compiler_params=_params(1),
        name="pre_cols_rope" if rope else "pre_cols",
    )(*args)


def _rope_tables(pos, rows):
    half = A_ROT // 2
    inv = ROPE_THETA ** (-jnp.arange(half, dtype=F32) * 2.0 / A_ROT)
    ang = pos.astype(F32)[:, None] * inv[None, :]
    cos, sin = jnp.cos(ang), jnp.sin(ang)
    t = pos.shape[0]
    c = jnp.concatenate([cos, cos, jnp.ones((t, HALF - A_ROT), F32)], axis=1)
    s1 = jnp.concatenate([-sin, jnp.zeros((t, HALF - half), F32)], axis=1)
    s2 = jnp.concatenate([jnp.zeros((t, half), F32), sin, jnp.zeros((t, HALF - A_ROT), F32)], axis=1)
    reps = (max(rows // t, 1), LANES // HALF)
    return tuple(jnp.tile(a, reps) for a in (c, s1, s2)), (cos.T, sin.T)


def _lo_lanes():
    return lax.broadcasted_iota(jnp.int32, (1, LANES), 1) < HALF


def _halves(q):
    lo = _lo_lanes()
    zero = jnp.zeros_like(q)
    return jnp.where(lo, q, zero), jnp.where(lo, zero, q)


def _lam(lam_ref, lam_init):
    a = jnp.sum(lam_ref[0:1, :] * lam_ref[1:2, :], axis=-1, keepdims=True)
    b = jnp.sum(lam_ref[2:3, :] * lam_ref[3:4, :], axis=-1, keepdims=True)
    return jnp.exp(a) - jnp.exp(b) + lam_init


def _diff_epilogue(o1, o2, lam_ref, subg_ref, g_ref, o_ref, lam_init):
    o = o1 - _lam(lam_ref, lam_init) * o2
    ms = jnp.mean(o * o, axis=-1, keepdims=True)
    o = (o * lax.rsqrt(ms + EPS)) * subg_ref[...] * (1.0 - lam_init)
    g = g_ref[...]
    o_ref[...] = (o * (g * _sigmoid(g))).astype(o_ref.dtype)


def _softplus2(z):
    sign_bit = jnp.uint32(1 << 31)
    neg_abs = lax.bitcast_convert_type(lax.bitcast_convert_type(z, jnp.uint32) | sign_bit, F32)
    return jnp.maximum(z, 0.0) + jnp.log2(1.0 + jnp.exp2(neg_abs))


def _stick_weights(z, valid, carry, tri):
    sp = _softplus2(z)
    if valid is not None:
        sp = jnp.where(valid, sp, 0.0)
    pieces = [None] * (z.shape[1] // TRI)
    for sb in range(len(pieces) - 1, -1, -1):
        cols = slice(sb * TRI, (sb + 1) * TRI)
        sps = sp[:, cols]
        suffix = jnp.dot(sps.astype(BF16), tri, preferred_element_type=F32)
        pieces[sb] = jnp.exp2(jnp.minimum(z[:, cols] - suffix, 0.0) - carry)
        carry = carry + suffix[:, :1]
    a = pieces[0] if len(pieces) == 1 else jnp.concatenate(pieces, axis=1)
    if valid is not None:
        a = jnp.where(valid, a, 0.0)
    return a, carry


def _kv_tiles(ref, kj, bk):
    per = bk // KV_TILE
    tiles = [ref[kj * per + i] for i in range(per)]
    return tiles[0] if per == 1 else jnp.concatenate(tiles, axis=1)


def _attn_a_prompt_body(lam_ref, subg_ref, q_ref, kt_ref, v_ref, g_ref, o_ref,
                        qz_ref, m_ref, l_ref, acc_ref, *, bq, bk, diag_slab, lam_init):
    qi = pl.program_id(2)
    q1, q2 = _halves(q_ref[...])
    qz_ref[0] = q1
    qz_ref[1] = q2
    m_ref[...] = jnp.full(m_ref.shape, NEG_INF, F32)
    l_ref[...] = jnp.zeros(l_ref.shape, F32)
    acc_ref[...] = jnp.zeros(acc_ref.shape, F32)

    def step(kj, diag):
        kt = _kv_tiles(kt_ref, kj, bk)
        v = v_ref[pl.ds(pl.multiple_of(kj * bk, bk), bk), :]
        if diag is not None:
            v = jnp.concatenate([v, jnp.ones_like(v)], axis=1)
        slab = bq if diag is None else diag_slab
        for r0 in range(0, bq, slab):
            if diag is None or _chunk_of((diag + 1) * bk - 1) <= _chunk_of(r0):
                vis = None
            elif _chunk_of(diag * bk) > _chunk_of(r0 + slab - 1):
                continue
            else:
                qpos = r0 + lax.broadcasted_iota(jnp.int32, (slab, bk), 0)
                kpos = diag * bk + lax.broadcasted_iota(jnp.int32, (slab, bk), 1)
                vis = _chunk_of(kpos) <= _chunk_of(qpos)
            rows = slice(r0, r0 + slab)
            for c in range(2):
                s = jnp.dot(qz_ref[c, rows, :], kt, preferred_element_type=F32)
                if vis is not None:
                    s = jnp.where(vis, s, NEG_INF)
                m_prev = m_ref[c, rows, :]
                m_new = jnp.maximum(m_prev, jnp.max(s, axis=-1, keepdims=True))
                alpha = jnp.exp2(m_prev - m_new)
                p = jnp.exp2(s - jnp.tile(m_new, (1, bk // LANES)))
                pv = jnp.dot(p.astype(BF16), v, preferred_element_type=F32)
                psum = jnp.sum(p, axis=-1, keepdims=True) if diag is None else pv[:, LANES:]
                l_ref[c, rows, :] = alpha * l_ref[c, rows, :] + psum
                acc_ref[c, rows, :] = alpha * acc_ref[c, rows, :] + pv[:, :LANES]
                m_ref[c, rows, :] = m_new

    n_full = qi * (bq // bk)

    def body(kj, c):
        step(kj, None)
        return c

    lax.fori_loop(0, n_full, body, 0)
    for r in range(bq // bk):
        step(n_full + r, r)

    _diff_epilogue(acc_ref[0] / l_ref[0], acc_ref[1] / l_ref[1],
                   lam_ref, subg_ref, g_ref, o_ref, lam_init)


def _attn_a_prompt(q, ktb, vb, gate, lam4, subg, lam_init):
    b, t, _ = q.shape
    bq, bk = A_BQ, A_BK
    assert bq % bk == 0 and bk % KV_TILE == 0 and t % bq == 0 and bk % CHUNK == 0
    q_spec = pl.BlockSpec((None, bq, LANES), lambda bi, h, qi: (bi, qi, h))
    kt_spec = pl.BlockSpec((None, t // KV_TILE, LANES, KV_TILE), lambda bi, h, qi: (bi, 0, h, 0))
    v_spec = pl.BlockSpec((None, t, LANES), lambda bi, h, qi: (bi, 0, h))
    full = lambda shape: pl.BlockSpec(shape, lambda bi, h, qi: (0,) * len(shape))
    return pl.pallas_call(
        functools.partial(_attn_a_prompt_body, bq=bq, bk=bk, diag_slab=A_SLAB, lam_init=lam_init),
        grid=(b, A_HEADS, t // bq),
        in_specs=[full(lam4.shape), full(subg.shape), q_spec, kt_spec, v_spec, q_spec],
        out_specs=q_spec,
        out_shape=jax.ShapeDtypeStruct((b, t, WIDTH), BF16),
        scratch_shapes=[pltpu.VMEM((2, bq, LANES), BF16),
                        pltpu.VMEM((2, bq, LANES), F32),
                        pltpu.VMEM((2, bq, LANES), F32),
                        pltpu.VMEM((2, bq, LANES), F32)],
        compiler_params=_params(3),
        name="attn_a_prompt",
    )(lam4, subg, q, ktb, vb, gate)


def _attn_b_prompt_body(tri_ref, q_ref, kt_ref, vt_ref, g_ref, o_ref,
                        qz_ref, carry_ref, acc_ref, *, bq, bk, diag_slab):
    qi = pl.program_id(2)
    q1, q2 = _halves(q_ref[...])
    qz_ref[0] = q1
    qz_ref[1] = q2
    carry_ref[...] = jnp.zeros(carry_ref.shape, F32)
    acc_ref[...] = jnp.zeros(acc_ref.shape, F32)

    def step(kj, diag):
        kt = _kv_tiles(kt_ref, kj, bk)
        vt = _kv_tiles(vt_ref, kj, bk)
        slab = bq if diag is None else diag_slab
        for r0 in range(0, bq, slab):
            if diag is None or (diag + 1) * bk <= r0:
                valid = None
            elif diag * bk >= r0 + slab - 1:
                continue
            else:
                qpos = r0 + lax.broadcasted_iota(jnp.int32, (slab, bk), 0)
                kpos = diag * bk + lax.broadcasted_iota(jnp.int32, (slab, bk), 1)
                valid = kpos < qpos
            rows = slice(r0, r0 + slab)
            zs = [jnp.dot(qz_ref[c, rows, :], kt, preferred_element_type=F32) for c in range(2)]
            sps = [_softplus2(z) for z in zs]
            if valid is not None:
                sps = [jnp.where(valid, sp, 0.0) for sp in sps]
            carries = [carry_ref[c, rows, :] for c in range(2)]
            pieces = [[None] * (bk // TRI) for _ in range(2)]
            for sb in range(bk // TRI - 1, -1, -1):
                cols = slice(sb * TRI, (sb + 1) * TRI)
                sfx = [jnp.dot(sp[:, cols].astype(BF16), tri_ref[...], preferred_element_type=F32) for sp in sps]
                for c in range(2):
                    pieces[c][sb] = jnp.exp2(jnp.minimum(zs[c][:, cols] - sfx[c], 0.0) - carries[c])
                    carries[c] = carries[c] + sfx[c][:, :1]
            for c in range(2):
                a = jnp.concatenate(pieces[c], axis=1)
                if valid is not None:
                    a = jnp.where(valid, a, 0.0)
                acc_ref[c, rows, :] += lax.dot_general(a.astype(BF16), vt, _NT, preferred_element_type=F32)
                carry_ref[c, rows, :] = carries[c]

    n_full = qi * (bq // bk)
    for r in range(bq // bk - 1, -1, -1):
        step(n_full + r, r)

    def exhausted():
        return jnp.min(carry_ref[...]) >= STICK_EXIT

    def cond(state):
        t, done = state
        return jnp.logical_and(t < n_full, jnp.logical_not(done))

    def body(state):
        t, _ = state
        step(n_full - 1 - t, None)
        return t + 1, exhausted()

    lax.while_loop(cond, body, (jnp.int32(0), exhausted()))

    g = g_ref[...]
    o = jnp.where(_lo_lanes(), acc_ref[0], acc_ref[1])
    o_ref[...] = (o * (g * _sigmoid(g))).astype(o_ref.dtype)


def _attn_b_prompt(q, ktb, vtb, gate, tri):
    b, t, _ = q.shape
    bq, bk = B_BQ, B_BK
    assert bq % bk == 0 and bk % KV_TILE == 0 and t % bq == 0 and bk % TRI == 0
    q_spec = pl.BlockSpec((None, bq, LANES), lambda bi, h, qi: (bi, qi, h))
    kt_spec = pl.BlockSpec((None, t // KV_TILE, LANES, KV_TILE), lambda bi, h, qi: (bi, 0, h, 0))
    return pl.pallas_call(
        functools.partial(_attn_b_prompt_body, bq=bq, bk=bk, diag_slab=B_SLAB),
        grid=(b, WIDTH // LANES, t // bq),
        in_specs=[pl.BlockSpec((TRI, TRI), lambda bi, h, qi: (0, 0)),
                  q_spec, kt_spec, kt_spec, q_spec],
        out_specs=q_spec,
        out_shape=jax.ShapeDtypeStruct((b, t, WIDTH), BF16),
        scratch_shapes=[pltpu.VMEM((2, bq, LANES), BF16),
                        pltpu.VMEM((2, bq, 1), F32),
                        pltpu.VMEM((2, bq, LANES), F32)],
        compiler_params=_params(3),
        name="attn_b_prompt",
    )(tri, q, ktb, vtb, gate)


def _pad_rows(x, rows):
    return jnp.concatenate([x, jnp.zeros((rows - x.shape[0], x.shape[1]), x.dtype)], axis=0)


def _attn_a_sample_body(lam_ref, subg_ref, q_ref, kn_ref, vn_ref, ktp_ref, vp_ref, g_ref, o_ref,
                        qq_ref, m_ref, l_ref, acc_ref, *, t_new, past, chunk, lam_init):
    pc = pl.program_id(1)

    @pl.when(pc == 0)
    def _():
        for h in range(A_HEADS):
            qq_ref[h] = jnp.concatenate(_halves(q_ref[:, h * LANES:(h + 1) * LANES]), axis=0)
        m_ref[...] = jnp.full(m_ref.shape, NEG_INF, F32)
        l_ref[...] = jnp.zeros(l_ref.shape, F32)
        acc_ref[...] = jnp.zeros(acc_ref.shape, F32)

    def update(h, s, v):
        m_prev = m_ref[h]
        m_new = jnp.maximum(m_prev, jnp.max(s, axis=-1, keepdims=True))
        alpha = jnp.exp2(m_prev - m_new)
        p = jnp.exp2(s - jnp.tile(m_new, (1, s.shape[1] // LANES)))
        l_ref[h] = alpha * l_ref[h] + jnp.sum(p, axis=-1, keepdims=True)
        acc_ref[h] = alpha * acc_ref[h] + jnp.dot(p.astype(BF16), v, preferred_element_type=F32)
        m_ref[h] = m_new

    for h in range(A_HEADS):
        kt = ktp_ref[h * LANES:(h + 1) * LANES, :].astype(BF16)
        v = vp_ref[pl.ds(h, chunk, stride=A_HEADS), :].astype(BF16)
        update(h, jnp.dot(qq_ref[h], kt, preferred_element_type=F32), v)

    @pl.when(pc == pl.num_programs(1) - 1)
    def _():
        row = lax.broadcasted_iota(jnp.int32, (2 * t_new, LANES), 0)
        col = lax.broadcasted_iota(jnp.int32, (2 * t_new, LANES), 1)
        vis = jnp.logical_and(col < t_new,
                              _chunk_of(past + col) <= _chunk_of(past + _mod_pow2(row, t_new)))
        for h in range(A_HEADS):
            cols = slice(h * LANES, (h + 1) * LANES)
            kn = _pad_rows(kn_ref[:, cols], LANES)
            vn = _pad_rows(vn_ref[:, cols], LANES)
            s_n = lax.dot_general(qq_ref[h], kn, _NT, preferred_element_type=F32)
            update(h, jnp.where(vis, s_n, NEG_INF), vn)
            o = acc_ref[h] / l_ref[h]
            _diff_epilogue(o[:t_new], o[t_new:], lam_ref, subg_ref, g_ref.at[:, cols], o_ref.at[:, cols],
                           lam_init)


def _attn_a_sample(q, kb, vb, kt_past, v_past, gate, lam4, subg, lam_init):
    b, t_new, _ = q.shape
    past = kt_past.shape[2]
    chunk = SAMPLE_CHUNK
    assert v_past.shape[1] == past * A_HEADS and past % chunk == 0
    new_spec = pl.BlockSpec((None, t_new, WIDTH), lambda bi, pc: (bi, 0, 0))
    ktp_spec = pl.BlockSpec((None, WIDTH, chunk), lambda bi, pc: (bi, 0, pc))
    vp_spec = pl.BlockSpec((None, chunk * A_HEADS, LANES), lambda bi, pc: (bi, pc, 0))
    full = lambda shape: pl.BlockSpec(shape, lambda bi, pc: (0,) * len(shape))
    state = pltpu.VMEM((A_HEADS, 2 * t_new, LANES), F32)
    return pl.pallas_call(
        functools.partial(_attn_a_sample_body, t_new=t_new, past=past, chunk=chunk, lam_init=lam_init),
        grid=(b, past // chunk),
        in_specs=[full(lam4.shape), full(subg.shape), new_spec, new_spec, new_spec,
                  ktp_spec, vp_spec, new_spec],
        out_specs=new_spec,
        out_shape=jax.ShapeDtypeStruct((b, t_new, WIDTH), BF16),
        scratch_shapes=[pltpu.VMEM((A_HEADS, 2 * t_new, LANES), BF16), state, state, state],
        compiler_params=_params(2),
        name="attn_a_sample",
    )(lam4, subg, q, kb, vb, kt_past, v_past, gate)


def _attn_b_sample_body(tri_ref, q_ref, kn_ref, vn_ref, ktp_ref, vtp_ref, g_ref, o_ref,
                        qq_ref, carry_ref, acc_ref, *, t_new, chunk):
    pc = pl.program_id(1)
    rows = 2 * t_new
    n_blk = chunk // TRI
    n_pairs = WIDTH // LANES
    tri = tri_ref[...]

    @pl.when(pc == 0)
    def _():
        row = lax.broadcasted_iota(jnp.int32, (rows, TRI), 0)
        col = lax.broadcasted_iota(jnp.int32, (rows, TRI), 1)
        valid = col < _mod_pow2(row, t_new)
        for hp in range(n_pairs):
            cols = slice(hp * LANES, (hp + 1) * LANES)
            qq = jnp.concatenate(_halves(q_ref[:, cols]), axis=0)
            qq_ref[hp] = qq
            kn = _pad_rows(kn_ref[:, cols], TRI)
            vn = _pad_rows(vn_ref[:, cols], TRI)
            z_n = lax.dot_general(qq, kn, _NT, preferred_element_type=F32)
            a_n, carry = _stick_weights(z_n, valid, jnp.zeros((rows, 1), F32), tri)
            acc_ref[hp] = jnp.dot(a_n.astype(BF16), vn, preferred_element_type=F32)
            carry_ref[hp] = carry

    @pl.when(jnp.min(carry_ref[...]) < STICK_EXIT)
    def _():
        for hp in range(n_pairs):
            feat = slice(hp * LANES, (hp + 1) * LANES)
            z_all = jnp.dot(qq_ref[hp], ktp_ref[feat, :].astype(BF16), preferred_element_type=F32)
            z = jnp.concatenate([z_all[:, c * TRI:(c + 1) * TRI] for c in range(n_blk)], axis=0)
            suffix = jnp.dot(_softplus2(z).astype(BF16), tri, preferred_element_type=F32)
            running = carry_ref[hp]
            carries = [None] * n_blk
            for c in range(n_blk - 1, -1, -1):
                carries[c] = running
                running = running + suffix[c * rows:(c + 1) * rows, :1]
            carry_ref[hp] = running
            a = jnp.exp2(jnp.minimum(z - suffix, 0.0) - jnp.concatenate(carries, axis=0)).astype(BF16)
            vtp = vtp_ref[feat, :].astype(BF16)
            acc = acc_ref[hp]
            for c in range(n_blk):
                acc = acc + lax.dot_general(a[c * rows:(c + 1) * rows], vtp[:, c * TRI:(c + 1) * TRI], _NT,
                                            preferred_element_type=F32)
            acc_ref[hp] = acc

    @pl.when(pc == pl.num_programs(1) - 1)
    def _():
        for hp in range(n_pairs):
            cols = slice(hp * LANES, (hp + 1) * LANES)
            acc = acc_ref[hp]
            g = g_ref[:, cols]
            o = jnp.where(_lo_lanes(), acc[:t_new], acc[t_new:])
            o_ref[:, cols] = (o * (g * _sigmoid(g))).astype(o_ref.dtype)


def _attn_b_sample(q, kb, vb, kt_past, vt_past, gate, tri):
    b, t_new, _ = q.shape
    past = kt_past.shape[2]
    chunk = SAMPLE_CHUNK
    assert past % chunk == 0 and chunk % TRI == 0 and t_new <= TRI
    n_pc = past // chunk
    n_pairs = WIDTH // LANES
    new_spec = pl.BlockSpec((None, t_new, WIDTH), lambda bi, pc: (bi, 0, 0))
    past_spec = pl.BlockSpec((None, WIDTH, chunk), lambda bi, pc: (bi, 0, n_pc - 1 - pc))
    return pl.pallas_call(
        functools.partial(_attn_b_sample_body, t_new=t_new, chunk=chunk),
        grid=(b, n_pc),
        in_specs=[pl.BlockSpec((TRI, TRI), lambda bi, pc: (0, 0)),
                  new_spec, new_spec, new_spec, past_spec, past_spec, new_spec],
        out_specs=new_spec,
        out_shape=jax.ShapeDtypeStruct((b, t_new, WIDTH), BF16),
        scratch_shapes=[pltpu.VMEM((n_pairs, 2 * t_new, LANES), BF16),
                        pltpu.VMEM((n_pairs, 2 * t_new, 1), F32),
                        pltpu.VMEM((n_pairs, 2 * t_new, LANES), F32)],
        compiler_params=_params(2),
        name="attn_b_sample",
    )(tri, q, kb, vb, kt_past, vt_past, gate)


def _post_body(*refs, final):
    if final:
        o_ref, x_ref, p_ref, wo_ref, wg_ref, wp_ref, fg_ref, out_ref = refs
    else:
        o_ref, x_ref, p_ref, wo_ref, wg_ref, wp_ref, out_ref = refs
    x = x_ref[...] + jnp.dot(o_ref[...], wo_ref[...], preferred_element_type=F32)
    gate = _sigmoid(jnp.dot(x.astype(BF16), wg_ref[...], preferred_element_type=F32))
    x = x + gate * jnp.dot(p_ref[...].astype(BF16), wp_ref[...], preferred_element_type=F32)
    if final:
        ms = jnp.mean(x * x, axis=-1, keepdims=True)
        x = (x * lax.rsqrt(ms + EPS)) * fg_ref[...]
    out_ref[...] = x


def _post(o, x, p, layer, w_out, w_gate, w_proj, final_g):
    n = x.shape[0]
    rows = PRE_ROWS
    final = final_g is not None
    row_spec = lambda width: pl.BlockSpec((rows, width), lambda i: (i, 0))
    full = lambda shape: pl.BlockSpec(shape, lambda i: (0, 0))
    in_specs = [row_spec(WIDTH), row_spec(D_MODEL),
                pl.BlockSpec((None, rows, PLE_DIM), lambda i: (layer, i, 0)),
                full(w_out.shape), full(w_gate.shape), full(w_proj.shape)]
    args = [o, x, p, w_out, w_gate, w_proj]
    if final:
        in_specs.append(full((1, D_MODEL)))
        args.append(final_g.reshape(1, D_MODEL))
    return pl.pallas_call(
        functools.partial(_post_body, final=final),
        grid=(n // rows,),
        in_specs=in_specs,
        out_specs=row_spec(D_MODEL),
        out_shape=jax.ShapeDtypeStruct((n, D_MODEL), F32),
        compiler_params=_params(1),
        name="post_final" if final else "post",
    )(*args)


def _mixer_order(depth):
    return [("a", i // 2) if i % 2 == 0 else ("b", i // 2) for i in range(depth)]


def _lam_init(i):
    return 0.8 - 0.6 * math.exp(-0.3 * i)


def _trunk_prompt(x, p, pos, weights):
    (a_norm_g, a_w_in, a_lam, a_subln_g, a_w_out, b_norm_g, b_w_in, b_w_out,
     ple_w_proj, ple_w_gate, final_norm_g) = weights
    b, t, _ = x.shape
    n = b * t
    depth = p.shape[0]
    xf = x.reshape(n, D_MODEL)
    tables = _rope_tables(pos, PRE_ROWS)
    tri = jnp.tri(TRI, dtype=BF16)
    shp = (b, t, WIDTH)
    ak, av, bk, bv = [], [], [], []
    for i, (kind, j) in enumerate(_mixer_order(depth)):
        if kind == "a":
            q, kt, ktb, v, vb, gate = _pre_cols(xf, b, t, a_norm_g[j], a_w_in[j], tables, False)
            o = _attn_a_prompt(q.reshape(shp), ktb, vb.reshape(shp), gate.reshape(shp),
                               a_lam[j], a_subln_g[j].reshape(1, 2 * A_DH), _lam_init(i))
            ak.append(kt.reshape(b, A_HEADS, 2, A_DH, t).transpose(0, 4, 1, 2, 3))
            av.append(v.reshape(b, t, A_HEADS, 2 * A_DH))
            w_out = a_w_out[j]
        else:
            q, kt, ktb, vt, vtb, gate = _pre_cols(xf, b, t, b_norm_g[j], b_w_in[j], None, True)
            o = _attn_b_prompt(q.reshape(shp), ktb, vtb, gate.reshape(shp), tri)
            bk.append(kt.reshape(b, B_HEADS, B_DH, t).transpose(0, 3, 1, 2))
            bv.append(vt.reshape(b, B_HEADS, B_DH, t).transpose(0, 3, 1, 2))
            w_out = b_w_out[j]
        xf = _post(o.reshape(n, WIDTH), xf, p.reshape(depth, n, PLE_DIM), i, w_out,
                   ple_w_gate[i], ple_w_proj[i], final_norm_g if i == depth - 1 else None)
    return (xf.reshape(b, t, D_MODEL), jnp.stack(ak), jnp.stack(av), jnp.stack(bk), jnp.stack(bv))


def _trunk_sample(x, p, pos, weights, caches):
    (a_norm_g, a_w_in, a_lam, a_subln_g, a_w_out, b_norm_g, b_w_in, b_w_out,
     ple_w_proj, ple_w_gate, final_norm_g) = weights
    cache_a_k, cache_a_v, cache_b_k, cache_b_v = caches
    b, t, _ = x.shape
    n = b * t
    depth = p.shape[0]
    past = cache_a_k.shape[2]
    xf = x.reshape(n, D_MODEL)
    lane_tables, _ = _rope_tables(pos, PRE_ROWS)
    tri = jnp.tri(TRI, dtype=BF16)
    shp = (b, t, WIDTH)
    ak, av, bk, bv = [], [], [], []
    for i, (kind, j) in enumerate(_mixer_order(depth)):
        if kind == "a":
            q, k, kb, v, vb, gate = _pre_rows(xf, a_norm_g[j], a_w_in[j], lane_tables)
            kt_past = cache_a_k[j].transpose(0, 2, 3, 4, 1).reshape(b, WIDTH, past)
            v_past = cache_a_v[j].reshape(b, past * A_HEADS, 2 * A_DH)
            o = _attn_a_sample(q.reshape(shp), kb.reshape(shp), vb.reshape(shp), kt_past, v_past,
                               gate.reshape(shp), a_lam[j], a_subln_g[j].reshape(1, 2 * A_DH), _lam_init(i))
            ak.append(k.reshape(b, t, A_HEADS, 2, A_DH))
            av.append(v.reshape(b, t, A_HEADS, 2 * A_DH))
            w_out = a_w_out[j]
        else:
            q, k, kb, v, vb, gate = _pre_rows(xf, b_norm_g[j], b_w_in[j], None)
            kt_past = cache_b_k[j].transpose(0, 2, 3, 1).reshape(b, WIDTH, past)
            vt_past = cache_b_v[j].transpose(0, 2, 3, 1).reshape(b, WIDTH, past)
            o = _attn_b_sample(q.reshape(shp), kb.reshape(shp), vb.reshape(shp), kt_past, vt_past,
                               gate.reshape(shp), tri)
            bk.append(k.reshape(b, t, B_HEADS, B_DH))
            bv.append(v.reshape(b, t, B_HEADS, B_DH))
            w_out = b_w_out[j]
        xf = _post(o.reshape(n, WIDTH), xf, p.reshape(depth, n, PLE_DIM), i, w_out,
                   ple_w_gate[i], ple_w_proj[i], final_norm_g if i == depth - 1 else None)
    return (xf.reshape(b, t, D_MODEL), jnp.stack(ak), jnp.stack(av), jnp.stack(bk), jnp.stack(bv))


def kernel(x_prompt, x_sample, cache_a_k, cache_a_v, cache_b_k, cache_b_v, p_prompt, p_sample,
           a_norm_g, a_w_in, a_lam_q1, a_lam_k1, a_lam_q2, a_lam_k2, a_subln_g, a_w_out,
           b_norm_g, b_w_in, b_w_out, ple_w_proj, ple_w_gate, final_norm_g):
    past = cache_a_k.shape[2]
    a_lam = jnp.stack([a_lam_q1, a_lam_k1, a_lam_q2, a_lam_k2], axis=1)
    weights = (a_norm_g, a_w_in.astype(BF16), a_lam, a_subln_g, a_w_out.astype(BF16),
               b_norm_g, b_w_in.astype(BF16), b_w_out.astype(BF16),
               ple_w_proj.astype(BF16), ple_w_gate.astype(BF16), final_norm_g)
    pos_p = jnp.arange(x_prompt.shape[1], dtype=jnp.int32)
    pos_s = past + jnp.arange(x_sample.shape[1], dtype=jnp.int32)
    y_p, ak_p, av_p, bk_p, bv_p = _trunk_prompt(x_prompt, p_prompt, pos_p, weights)
    y_s, ak_s, av_s, bk_s, bv_s = _trunk_sample(x_sample, p_sample, pos_s, weights,
                                                (cache_a_k, cache_a_v, cache_b_k, cache_b_v))
    return (y_p, y_s, ak_p, av_p, bk_p, bv_p, ak_s, av_s, bk_s, bv_s)
```

```python
import functools
import math

import jax
import jax.numpy as jnp
from jax import lax
from jax.experimental import pallas as pl
from jax.experimental.pallas import tpu as pltpu

F32 = jnp.float32
BF16 = jnp.bfloat16

D_MODEL = 1024
CHUNK = 64
PLE_DIM = 256
ROPE_THETA = 500000.0
EPS = 1e-6
NEG_INF = -1e30
A_HEADS = 8
A_DH = 64
A_ROT = A_DH // 4
B_HEADS = 16
B_DH = 64
WIDTH = 1024
LANES = 128
SUBLANES = 8
HALF = 64
Q_SCALE = A_DH ** -0.5 * math.log2(math.e)

PRE_ROWS = 256
KV_TILE = 256
TRI = 256
A_BQ, A_BK = 2048, 512
B_BQ, B_BK = 256, 256
A_SLAB, B_SLAB = 512, 256
STICK_EXIT = 151.0
SAMPLE_B_CHUNK = 256
SAMPLE_CHUNK = 2048
VMEM_LIMIT = 48 * 1024 * 1024

_NT = (((1,), (1,)), ((), ()))


def _params(n_axes, vmem_limit=VMEM_LIMIT):
    return pltpu.CompilerParams(
        dimension_semantics=("arbitrary",) * n_axes, vmem_limit_bytes=vmem_limit)


def _sigmoid(x):
    return 1.0 / (1.0 + jnp.exp(-x))


def _mod_pow2(x, n):
    assert n & (n - 1) == 0
    return x & (n - 1)


def _chunk_of(pos):
    assert CHUNK & (CHUNK - 1) == 0
    return pos >> (CHUNK.bit_length() - 1)


def _rope_lanes(sl, c_ref, s1_ref, s2_ref):
    return (sl * c_ref[...] + pltpu.roll(sl, LANES - A_ROT // 2, 1) * s1_ref[...]
            + pltpu.roll(sl, A_ROT // 2, 1) * s2_ref[...])


def _normed(x_ref, g_ref):
    x = x_ref[...]
    ms = jnp.mean(x * x, axis=-1, keepdims=True)
    return ((x * lax.rsqrt(ms + EPS)) * g_ref[...]).astype(BF16)


def _pre_rows_body(*refs, rope):
    if rope:
        x_ref, g_ref, wq_ref, wk_ref, wv_ref, wg_ref, c_ref, s1_ref, s2_ref = refs[:9]
        outs = refs[9:]
    else:
        x_ref, g_ref, wq_ref, wk_ref, wv_ref, wg_ref = refs[:6]
        outs = refs[6:]
    q_ref, k_ref, kb_ref, v_ref, vb_ref, gate_ref = outs
    hb = _normed(x_ref, g_ref)

    def rot(r, j):
        sl = r[:, j * LANES:(j + 1) * LANES]
        return _rope_lanes(sl, c_ref, s1_ref, s2_ref) if rope else sl

    q = jnp.dot(hb, wq_ref[...], preferred_element_type=F32)
    for j in range(WIDTH // LANES):
        q_ref[:, j * LANES:(j + 1) * LANES] = (rot(q, j) * Q_SCALE).astype(BF16)
    k = jnp.dot(hb, wk_ref[...], preferred_element_type=F32)
    for j in range(WIDTH // LANES):
        kr = rot(k, j)
        k_ref[:, j * LANES:(j + 1) * LANES] = kr
        kb_ref[:, j * LANES:(j + 1) * LANES] = kr.astype(BF16)
    v = jnp.dot(hb, wv_ref[...], preferred_element_type=F32)
    v_ref[...] = v
    vb_ref[...] = v.astype(BF16)
    gate_ref[...] = jnp.dot(hb, wg_ref[...], preferred_element_type=F32)


def _pre_cols_body(*refs, rope, v_cols):
    if rope:
        x_ref, g_ref, wq_ref, wkt_ref, wv_ref, wg_ref, c_ref, s1_ref, s2_ref, ct_ref, st_ref = refs[:11]
        outs = refs[11:]
    else:
        x_ref, g_ref, wq_ref, wkt_ref, wv_ref, wg_ref = refs[:6]
        outs = refs[6:]
    q_ref, kt_ref, ktb_ref, v_ref, vb_ref, gate_ref = outs
    hb = _normed(x_ref, g_ref)

    q = jnp.dot(hb, wq_ref[...], preferred_element_type=F32)
    for j in range(WIDTH // LANES):
        sl = q[:, j * LANES:(j + 1) * LANES]
        if rope:
            sl = _rope_lanes(sl, c_ref, s1_ref, s2_ref)
        q_ref[:, j * LANES:(j + 1) * LANES] = (sl * Q_SCALE).astype(BF16)

    kt = lax.dot_general(wkt_ref[...], hb, _NT, preferred_element_type=F32)
    if rope:
        half = A_ROT // 2
        assert half == SUBLANES
        cos, sin = ct_ref[...], st_ref[...]
        for grp in range(WIDTH // A_DH):
            r0 = grp * A_DH
            x0, x1 = kt[r0:r0 + half], kt[r0 + half:r0 + 2 * half]
            y0, y1 = x0 * cos - x1 * sin, x1 * cos + x0 * sin
            kt_ref[r0:r0 + half, :] = y0
            kt_ref[r0 + half:r0 + 2 * half, :] = y1
            kt_ref[r0 + 2 * half:r0 + A_DH, :] = kt[r0 + 2 * half:r0 + A_DH]
            ktb_ref[r0:r0 + A_DH, :] = jnp.concatenate(
                [y0, y1, kt[r0 + 2 * half:r0 + A_DH]], axis=0).astype(BF16)
    else:
        kt_ref[...] = kt
        ktb_ref[...] = kt.astype(BF16)

    if v_cols:
        vt = lax.dot_general(wv_ref[...], hb, _NT, preferred_element_type=F32)
        v_ref[...] = vt
        vb_ref[...] = vt.astype(BF16)
    else:
        v = jnp.dot(hb, wv_ref[...], preferred_element_type=F32)
        v_ref[...] = v
        vb_ref[...] = v.astype(BF16)
    gate_ref[...] = jnp.dot(hb, wg_ref[...], preferred_element_type=F32)


def _split_w(w_in, t_k, t_v):
    parts = [w_in[:, i * WIDTH:(i + 1) * WIDTH] for i in range(4)]
    if t_k:
        parts[1] = parts[1].T
    if t_v:
        parts[2] = parts[2].T
    return parts


def _pre_rows(x, norm_g, w_in, tables):
    n = x.shape[0]
    rows = PRE_ROWS
    rope = tables is not None
    row_spec = lambda width: pl.BlockSpec((rows, width), lambda i: (i, 0))
    full = lambda shape: pl.BlockSpec(shape, lambda i: (0, 0))
    in_specs = [row_spec(D_MODEL), full((1, D_MODEL))] + [full((D_MODEL, WIDTH))] * 4
    args = [x, norm_g.reshape(1, D_MODEL)] + _split_w(w_in, False, False)
    if rope:
        in_specs += [full((rows, LANES))] * 3
        args += list(tables)
    out_shape = [jax.ShapeDtypeStruct((n, WIDTH), dt) for dt in (BF16, F32, BF16, F32, BF16, F32)]
    return pl.pallas_call(
        functools.partial(_pre_rows_body, rope=rope),
        grid=(n // rows,),
        in_specs=in_specs,
        out_specs=[row_spec(WIDTH)] * 6,
        out_shape=out_shape,
        compiler_params=_params(1),
        name="pre_rows_rope" if rope else "pre_rows",
    )(*args)


def _pre_cols(x, b, t, norm_g, w_in, tables, v_cols):
    rows = PRE_ROWS
    assert rows == KV_TILE and t % rows == 0
    n_t = t // rows
    rope = tables is not None
    row_spec = lambda width: pl.BlockSpec((rows, width), lambda i: (i, 0))
    full = lambda shape: pl.BlockSpec(shape, lambda i: (0, 0))
    colf_spec = pl.BlockSpec((None, WIDTH, rows), lambda i: (i // n_t, 0, i % n_t))
    colb_spec = pl.BlockSpec((None, None, WIDTH, rows), lambda i: (i // n_t, i % n_t, 0, 0))
    in_specs = [row_spec(D_MODEL), full((1, D_MODEL))] + [full((D_MODEL, WIDTH))] * 4
    args = [x, norm_g.reshape(1, D_MODEL)] + _split_w(w_in, True, v_cols)
    if rope:
        lanes_tab, rows_tab = tables
        in_specs += [pl.BlockSpec((rows, LANES), lambda i: (i % n_t, 0))] * 3
        in_specs += [pl.BlockSpec((A_ROT // 2, rows), lambda i: (0, i % n_t))] * 2
        args += list(lanes_tab) + list(rows_tab)
    colf = jax.ShapeDtypeStruct((b, WIDTH, t), F32)
    colb = jax.ShapeDtypeStruct((b, n_t, WIDTH, rows), BF16)
    rowf = jax.ShapeDtypeStruct((b * t, WIDTH), F32)
    rowb = jax.ShapeDtypeStruct((b * t, WIDTH), BF16)
    out_shape = [rowb, colf, colb] + ([colf, colb] if v_cols else [rowf, rowb]) + [rowf]
    out_specs = ([row_spec(WIDTH), colf_spec, colb_spec]
                 + ([colf_spec, colb_spec] if v_cols else [row_spec(WIDTH)] * 2) + [row_spec(WIDTH)])
    return pl.pallas_call(
        functools.partial(_pre_cols_body, rope=rope, v_cols=v_cols),
        grid=(b * n_t,),
        in_specs=in_specs,
        out_specs=out_specs,
        out_shape=out_shape,
        compiler_params=_params(1),
        name="pre_cols_rope" if rope else "pre_cols",
    )(*args)


def _rope_tables(pos, rows):
    half = A_ROT // 2
    inv = ROPE_THETA ** (-jnp.arange(half, dtype=F32) * 2.0 / A_ROT)
    ang = pos.astype(F32)[:, None] * inv[None, :]
    cos, sin = jnp.cos(ang), jnp.sin(ang)
    t = pos.shape[0]
    c = jnp.concatenate([cos, cos, jnp.ones((t, HALF - A_ROT), F32)], axis=1)
    s1 = jnp.concatenate([-sin, jnp.zeros((t, HALF - half), F32)], axis=1)
    s2 = jnp.concatenate([jnp.zeros((t, half), F32), sin, jnp.zeros((t, HALF - A_ROT), F32)], axis=1)
    reps = (max(rows // t, 1), LANES // HALF)
    return tuple(jnp.tile(a, reps) for a in (c, s1, s2)), (cos.T, sin.T)


def _lo_lanes():
    return lax.broadcasted_iota(jnp.int32, (1, LANES), 1) < HALF


def _halves(q):
    lo = _lo_lanes()
    zero = jnp.zeros_like(q)
    return jnp.where(lo, q, zero), jnp.where(lo, zero, q)


def _lam(lam_ref, lam_init):
    a = jnp.sum(lam_ref[0:1, :] * lam_ref[1:2, :], axis=-1, keepdims=True)
    b = jnp.sum(lam_ref[2:3, :] * lam_ref[3:4, :], axis=-1, keepdims=True)
    return jnp.exp(a) - jnp.exp(b) + lam_init


def _diff_epilogue(o1, o2, lam_ref, subg_ref, g_ref, o_ref, lam_init):
    o = o1 - _lam(lam_ref, lam_init) * o2
    ms = jnp.mean(o * o, axis=-1, keepdims=True)
    o = (o * lax.rsqrt(ms + EPS)) * subg_ref[...] * (1.0 - lam_init)
    g = g_ref[...]
    o_ref[...] = (o * (g * _sigmoid(g))).astype(o_ref.dtype)


def _softplus2(z):
    sign_bit = jnp.uint32(1 << 31)
    neg_abs = lax.bitcast_convert_type(lax.bitcast_convert_type(z, jnp.uint32) | sign_bit, F32)
    return jnp.maximum(z, 0.0) + jnp.log2(1.0 + jnp.exp2(neg_abs))


def _stick_weights(z, valid, carry, tri):
    sp = _softplus2(z)
    if valid is not None:
        sp = jnp.where(valid, sp, 0.0)
    pieces = [None] * (z.shape[1] // TRI)
    for sb in range(len(pieces) - 1, -1, -1):
        cols = slice(sb * TRI, (sb + 1) * TRI)
        sps = sp[:, cols]
        suffix = jnp.dot(sps.astype(BF16), tri, preferred_element_type=F32)
        pieces[sb] = jnp.exp2(jnp.minimum(z[:, cols] - suffix, 0.0) - carry)
        carry = carry + suffix[:, :1]
    a = pieces[0] if len(pieces) == 1 else jnp.concatenate(pieces, axis=1)
    if valid is not None:
        a = jnp.where(valid, a, 0.0)
    return a, carry


def _kv_tiles(ref, kj, bk):
    per = bk // KV_TILE
    tiles = [ref[kj * per + i] for i in range(per)]
    return tiles[0] if per == 1 else jnp.concatenate(tiles, axis=1)


def _attn_a_prompt_body(lam_ref, subg_ref, q_ref, kt_ref, v_ref, g_ref, o_ref,
                        qz_ref, m_ref, l_ref, acc_ref, *, bq, bk, diag_slab, lam_init):
    qi = pl.program_id(2)
    q1, q2 = _halves(q_ref[...])
    qz_ref[0] = q1
    qz_ref[1] = q2
    m_ref[...] = jnp.full(m_ref.shape, NEG_INF, F32)
    l_ref[...] = jnp.zeros(l_ref.shape, F32)
    acc_ref[...] = jnp.zeros(acc_ref.shape, F32)

    def step(kj, diag):
        kt = _kv_tiles(kt_ref, kj, bk)
        v = v_ref[pl.ds(pl.multiple_of(kj * bk, bk), bk), :]
        if diag is not None:
            v = jnp.concatenate([v, jnp.ones_like(v)], axis=1)
        slab = bq if diag is None else diag_slab
        for r0 in range(0, bq, slab):
            if diag is None or _chunk_of((diag + 1) * bk - 1) <= _chunk_of(r0):
                vis = None
            elif _chunk_of(diag * bk) > _chunk_of(r0 + slab - 1):
                continue
            else:
                qpos = r0 + lax.broadcasted_iota(jnp.int32, (slab, bk), 0)
                kpos = diag * bk + lax.broadcasted_iota(jnp.int32, (slab, bk), 1)
                vis = _chunk_of(kpos) <= _chunk_of(qpos)
            rows = slice(r0, r0 + slab)
            for c in range(2):
                s = jnp.dot(qz_ref[c, rows, :], kt, preferred_element_type=F32)
                if vis is not None:
                    s = jnp.where(vis, s, NEG_INF)
                m_prev = m_ref[c, rows, :]
                m_new = jnp.maximum(m_prev, jnp.max(s, axis=-1, keepdims=True))
                alpha = jnp.exp2(m_prev - m_new)
                p = jnp.exp2(s - jnp.tile(m_new, (1, bk // LANES)))
                pv = jnp.dot(p.astype(BF16), v, preferred_element_type=F32)
                psum = jnp.sum(p, axis=-1, keepdims=True) if diag is None else pv[:, LANES:]
                l_ref[c, rows, :] = alpha * l_ref[c, rows, :] + psum
                acc_ref[c, rows, :] = alpha * acc_ref[c, rows, :] + pv[:, :LANES]
                m_ref[c, rows, :] = m_new

    n_full = qi * (bq // bk)

    def body(kj, c):
        step(kj, None)
        return c

    lax.fori_loop(0, n_full, body, 0)
    for r in range(bq // bk):
        step(n_full + r, r)

    _diff_epilogue(acc_ref[0] / l_ref[0], acc_ref[1] / l_ref[1],
                   lam_ref, subg_ref, g_ref, o_ref, lam_init)


def _attn_a_prompt(q, ktb, vb, gate, lam4, subg, lam_init):
    b, t, _ = q.shape
    bq, bk = A_BQ, A_BK
    assert bq % bk == 0 and bk % KV_TILE == 0 and t % bq == 0 and bk % CHUNK == 0
    q_spec = pl.BlockSpec((None, bq, LANES), lambda bi, h, qi: (bi, qi, h))
    kt_spec = pl.BlockSpec((None, t // KV_TILE, LANES, KV_TILE), lambda bi, h, qi: (bi, 0, h, 0))
    v_spec = pl.BlockSpec((None, t, LANES), lambda bi, h, qi: (bi, 0, h))
    full = lambda shape: pl.BlockSpec(shape, lambda bi, h, qi: (0,) * len(shape))
    return pl.pallas_call(
        functools.partial(_attn_a_prompt_body, bq=bq, bk=bk, diag_slab=A_SLAB, lam_init=lam_init),
        grid=(b, A_HEADS, t // bq),
        in_specs=[full(lam4.shape), full(subg.shape), q_spec, kt_spec, v_spec, q_spec],
        out_specs=q_spec,
        out_shape=jax.ShapeDtypeStruct((b, t, WIDTH), BF16),
        scratch_shapes=[pltpu.VMEM((2, bq, LANES), BF16),
                        pltpu.VMEM((2, bq, LANES), F32),
                        pltpu.VMEM((2, bq, LANES), F32),
                        pltpu.VMEM((2, bq, LANES), F32)],
        compiler_params=_params(3),
        name="attn_a_prompt",
    )(lam4, subg, q, ktb, vb, gate)


def _attn_b_prompt_body(tri_ref, q_ref, kt_ref, vt_ref, g_ref, o_ref,
                        qz_ref, carry_ref, acc_ref, *, bq, bk, diag_slab):
    qi = pl.program_id(2)
    q1, q2 = _halves(q_ref[...])
    qz_ref[0] = q1
    qz_ref[1] = q2
    carry_ref[...] = jnp.zeros(carry_ref.shape, F32)
    acc_ref[...] = jnp.zeros(acc_ref.shape, F32)

    def step(kj, diag):
        kt = _kv_tiles(kt_ref, kj, bk)
        vt = _kv_tiles(vt_ref, kj, bk)
        slab = bq if diag is None else diag_slab
        for r0 in range(0, bq, slab):
            if diag is None or (diag + 1) * bk <= r0:
                valid = None
            elif diag * bk >= r0 + slab - 1:
                continue
            else:
                qpos = r0 + lax.broadcasted_iota(jnp.int32, (slab, bk), 0)
                kpos = diag * bk + lax.broadcasted_iota(jnp.int32, (slab, bk), 1)
                valid = kpos < qpos
            rows = slice(r0, r0 + slab)
            zs = [jnp.dot(qz_ref[c, rows, :], kt, preferred_element_type=F32) for c in range(2)]
            sps = [_softplus2(z) for z in zs]
            if valid is not None:
                sps = [jnp.where(valid, sp, 0.0) for sp in sps]
            carries = [carry_ref[c, rows, :] for c in range(2)]
            pieces = [[None] * (bk // TRI) for _ in range(2)]
            for sb in range(bk // TRI - 1, -1, -1):
                cols = slice(sb * TRI, (sb + 1) * TRI)
                sfx = [jnp.dot(sp[:, cols].astype(BF16), tri_ref[...], preferred_element_type=F32) for sp in sps]
                for c in range(2):
                    pieces[c][sb] = jnp.exp2(jnp.minimum(zs[c][:, cols] - sfx[c], 0.0) - carries[c])
                    carries[c] = carries[c] + sfx[c][:, :1]
            for c in range(2):
                a = jnp.concatenate(pieces[c], axis=1)
                if valid is not None:
                    a = jnp.where(valid, a, 0.0)
                acc_ref[c, rows, :] += lax.dot_general(a.astype(BF16), vt, _NT, preferred_element_type=F32)
                carry_ref[c, rows, :] = carries[c]

    n_full = qi * (bq // bk)
    for r in range(bq // bk - 1, -1, -1):
        step(n_full + r, r)

    def exhausted():
        return jnp.min(carry_ref[...]) >= STICK_EXIT

    def cond(state):
        t, done = state
        return jnp.logical_and(t < n_full, jnp.logical_not(done))

    def body(state):
        t, _ = state
        step(n_full - 1 - t, None)
        return t + 1, exhausted()

    lax.while_loop(cond, body, (jnp.int32(0), exhausted()))

    g = g_ref[...]
    o = jnp.where(_lo_lanes(), acc_ref[0], acc_ref[1])
    o_ref[...] = (o * (g * _sigmoid(g))).astype(o_ref.dtype)


def _attn_b_prompt(q, ktb, vtb, gate, tri):
    b, t, _ = q.shape
    bq, bk = B_BQ, B_BK
    assert bq % bk == 0 and bk % KV_TILE == 0 and t % bq == 0 and bk % TRI == 0
    q_spec = pl.BlockSpec((None, bq, LANES), lambda bi, h, qi: (bi, qi, h))
    kt_spec = pl.BlockSpec((None, t // KV_TILE, LANES, KV_TILE), lambda bi, h, qi: (bi, 0, h, 0))
    return pl.pallas_call(
        functools.partial(_attn_b_prompt_body, bq=bq, bk=bk, diag_slab=B_SLAB),
        grid=(b, WIDTH // LANES, t // bq),
        in_specs=[pl.BlockSpec((TRI, TRI), lambda bi, h, qi: (0, 0)),
                  q_spec, kt_spec, kt_spec, q_spec],
        out_specs=q_spec,
        out_shape=jax.ShapeDtypeStruct((b, t, WIDTH), BF16),
        scratch_shapes=[pltpu.VMEM((2, bq, LANES), BF16),
                        pltpu.VMEM((2, bq, 1), F32),
                        pltpu.VMEM((2, bq, LANES), F32)],
        compiler_params=_params(3),
        name="attn_b_prompt",
    )(tri, q, ktb, vtb, gate)


def _pad_rows(x, rows):
    return jnp.concatenate([x, jnp.zeros((rows - x.shape[0], x.shape[1]), x.dtype)], axis=0)


def _attn_a_sample_body(lam_ref, subg_ref, q_ref, kn_ref, vn_ref, ktp_ref, vp_ref, g_ref, o_ref,
                        qq_ref, m_ref, l_ref, acc_ref, *, t_new, past, chunk, lam_init):
    pc = pl.program_id(1)

    @pl.when(pc == 0)
    def _():
        for h in range(A_HEADS):
            qq_ref[h] = jnp.concatenate(_halves(q_ref[:, h * LANES:(h + 1) * LANES]), axis=0)
        m_ref[...] = jnp.full(m_ref.shape, NEG_INF, F32)
        l_ref[...] = jnp.zeros(l_ref.shape, F32)
        acc_ref[...] = jnp.zeros(acc_ref.shape, F32)

    def update(h, s, v):
        m_prev = m_ref[h]
        m_new = jnp.maximum(m_prev, jnp.max(s, axis=-1, keepdims=True))
        alpha = jnp.exp2(m_prev - m_new)
        p = jnp.exp2(s - jnp.tile(m_new, (1, s.shape[1] // LANES)))
        l_ref[h] = alpha * l_ref[h] + jnp.sum(p, axis=-1, keepdims=True)
        acc_ref[h] = alpha * acc_ref[h] + jnp.dot(p.astype(BF16), v, preferred_element_type=F32)
        m_ref[h] = m_new

    for h in range(A_HEADS):
        kt = ktp_ref[h * LANES:(h + 1) * LANES, :].astype(BF16)
        v = vp_ref[pl.ds(h, chunk, stride=A_HEADS), :].astype(BF16)
        update(h, jnp.dot(qq_ref[h], kt, preferred_element_type=F32), v)

    @pl.when(pc == pl.num_programs(1) - 1)
    def _():
        row = lax.broadcasted_iota(jnp.int32, (2 * t_new, LANES), 0)
        col = lax.broadcasted_iota(jnp.int32, (2 * t_new, LANES), 1)
        vis = jnp.logical_and(col < t_new,
                              _chunk_of(past + col) <= _chunk_of(past + _mod_pow2(row, t_new)))
        for h in range(A_HEADS):
            cols = slice(h * LANES, (h + 1) * LANES)
            kn = _pad_rows(kn_ref[:, cols], LANES)
            vn = _pad_rows(vn_ref[:, cols], LANES)
            s_n = lax.dot_general(qq_ref[h], kn, _NT, preferred_element_type=F32)
            update(h, jnp.where(vis, s_n, NEG_INF), vn)
            o = acc_ref[h] / l_ref[h]
            _diff_epilogue(o[:t_new], o[t_new:], lam_ref, subg_ref, g_ref.at[:, cols], o_ref.at[:, cols],
                           lam_init)


def _attn_a_sample(q, kb, vb, kt_past, v_past, gate, lam4, subg, lam_init):
    b, t_new, _ = q.shape
    past = kt_past.shape[2]
    chunk = SAMPLE_CHUNK
    assert v_past.shape[1] == past * A_HEADS and past % chunk == 0
    new_spec = pl.BlockSpec((None, t_new, WIDTH), lambda bi, pc: (bi, 0, 0))
    ktp_spec = pl.BlockSpec((None, WIDTH, chunk), lambda bi, pc: (bi, 0, pc))
    vp_spec = pl.BlockSpec((None, chunk * A_HEADS, LANES), lambda bi, pc: (bi, pc, 0))
    full = lambda shape: pl.BlockSpec(shape, lambda bi, pc: (0,) * len(shape))
    state = pltpu.VMEM((A_HEADS, 2 * t_new, LANES), F32)
    return pl.pallas_call(
        functools.partial(_attn_a_sample_body, t_new=t_new, past=past, chunk=chunk, lam_init=lam_init),
        grid=(b, past // chunk),
        in_specs=[full(lam4.shape), full(subg.shape), new_spec, new_spec, new_spec,
                  ktp_spec, vp_spec, new_spec],
        out_specs=new_spec,
        out_shape=jax.ShapeDtypeStruct((b, t_new, WIDTH), BF16),
        scratch_shapes=[pltpu.VMEM((A_HEADS, 2 * t_new, LANES), BF16), state, state, state],
        compiler_params=_params(2),
        name="attn_a_sample",
    )(lam4, subg, q, kb, vb, kt_past, v_past, gate)


def _attn_b_sample_body(tri_ref, q_ref, kn_ref, vn_ref, g_ref, kt_hbm, vt_hbm, o_ref,
                        kbuf, vbuf, sems, qq_ref, carry_ref, acc_ref, *, t_new, past, chunk):
    bi = pl.program_id(0)
    rows = 2 * t_new
    n_blk = chunk // TRI
    n_chunks = past // chunk
    n_pairs = WIDTH // LANES
    slot = lax.rem(bi, 2)
    tri = tri_ref[...]

    def fetch(batch, ci, into):
        cols = pl.ds(pl.multiple_of(ci * chunk, chunk), chunk)
        return (pltpu.make_async_copy(kt_hbm.at[batch, :, cols], kbuf.at[into], sems.at[0, into]),
                pltpu.make_async_copy(vt_hbm.at[batch, :, cols], vbuf.at[into], sems.at[1, into]))

    @pl.when(bi == 0)
    def _():
        for cp in fetch(0, n_chunks - 1, 0):
            cp.start()

    @pl.when(bi + 1 < pl.num_programs(0))
    def _():
        for cp in fetch(bi + 1, n_chunks - 1, 1 - slot):
            cp.start()

    row = lax.broadcasted_iota(jnp.int32, (rows, TRI), 0)
    col = lax.broadcasted_iota(jnp.int32, (rows, TRI), 1)
    valid = col < _mod_pow2(row, t_new)
    for hp in range(n_pairs):
        cols = slice(hp * LANES, (hp + 1) * LANES)
        qq = jnp.concatenate(_halves(q_ref[:, cols]), axis=0)
        qq_ref[hp] = qq
        kn = _pad_rows(kn_ref[:, cols], TRI)
        vn = _pad_rows(vn_ref[:, cols], TRI)
        z_n = lax.dot_general(qq, kn, _NT, preferred_element_type=F32)
        a_n, carry = _stick_weights(z_n, valid, jnp.zeros((rows, 1), F32), tri)
        acc_ref[hp] = jnp.dot(a_n.astype(BF16), vn, preferred_element_type=F32)
        carry_ref[hp] = carry

    def absorb():
        for hp in range(n_pairs):
            feat = pl.ds(hp * LANES, LANES)
            z_all = jnp.dot(qq_ref[hp], kbuf[slot, feat, :].astype(BF16), preferred_element_type=F32)
            z = jnp.concatenate([z_all[:, c * TRI:(c + 1) * TRI] for c in range(n_blk)], axis=0)
            suffix = jnp.dot(_softplus2(z).astype(BF16), tri, preferred_element_type=F32)
            running = carry_ref[hp]
            carries = [None] * n_blk
            for c in range(n_blk - 1, -1, -1):
                carries[c] = running
                running = running + suffix[c * rows:(c + 1) * rows, :1]
            carry_ref[hp] = running
            a = jnp.exp2(jnp.minimum(z - suffix, 0.0) - jnp.concatenate(carries, axis=0)).astype(BF16)
            vtp = vbuf[slot, feat, :].astype(BF16)
            acc = acc_ref[hp]
            for c in range(n_blk):
                acc = acc + lax.dot_general(a[c * rows:(c + 1) * rows], vtp[:, c * TRI:(c + 1) * TRI], _NT,
                                            preferred_element_type=F32)
            acc_ref[hp] = acc

    def exhausted():
        return jnp.min(carry_ref[...]) >= STICK_EXIT

    for cp in fetch(bi, n_chunks - 1, slot):
        cp.wait()
    absorb()

    def cond(state):
        ci, done = state
        return jnp.logical_and(ci >= 0, jnp.logical_not(done))

    def body(state):
        ci, _ = state
        copies = fetch(bi, ci, slot)
        for cp in copies:
            cp.start()
        for cp in copies:
            cp.wait()
        absorb()
        return ci - 1, exhausted()

    lax.while_loop(cond, body, (jnp.int32(n_chunks - 2), exhausted()))

    for hp in range(n_pairs):
        cols = slice(hp * LANES, (hp + 1) * LANES)
        acc = acc_ref[hp]
        g = g_ref[:, cols]
        o = jnp.where(_lo_lanes(), acc[:t_new], acc[t_new:])
        o_ref[:, cols] = (o * (g * _sigmoid(g))).astype(o_ref.dtype)


def _attn_b_sample(q, kb, vb, kt_past, vt_past, gate, tri):
    b, t_new, _ = q.shape
    past = kt_past.shape[2]
    chunk = SAMPLE_B_CHUNK
    assert past % chunk == 0 and chunk % TRI == 0 and t_new <= TRI
    n_pairs = WIDTH // LANES
    new_spec = pl.BlockSpec((None, t_new, WIDTH), lambda bi: (bi, 0, 0))
    hbm_spec = pl.BlockSpec(memory_space=pl.ANY)
    return pl.pallas_call(
        functools.partial(_attn_b_sample_body, t_new=t_new, past=past, chunk=chunk),
        grid=(b,),
        in_specs=[pl.BlockSpec((TRI, TRI), lambda bi: (0, 0)),
                  new_spec, new_spec, new_spec, new_spec, hbm_spec, hbm_spec],
        out_specs=new_spec,
        out_shape=jax.ShapeDtypeStruct((b, t_new, WIDTH), BF16),
        scratch_shapes=[pltpu.VMEM((2, WIDTH, chunk), F32),
                        pltpu.VMEM((2, WIDTH, chunk), F32),
                        pltpu.SemaphoreType.DMA((2, 2)),
                        pltpu.VMEM((n_pairs, 2 * t_new, LANES), BF16),
                        pltpu.VMEM((n_pairs, 2 * t_new, 1), F32),
                        pltpu.VMEM((n_pairs, 2 * t_new, LANES), F32)],
        compiler_params=_params(1),
        name="attn_b_sample",
    )(tri, q, kb, vb, gate, kt_past, vt_past)


def _post_body(*refs, final):
    if final:
        o_ref, x_ref, p_ref, wo_ref, wg_ref, wp_ref, fg_ref, out_ref = refs
    else:
        o_ref, x_ref, p_ref, wo_ref, wg_ref, wp_ref, out_ref = refs
    x = x_ref[...] + jnp.dot(o_ref[...], wo_ref[...], preferred_element_type=F32)
    gate = _sigmoid(jnp.dot(x.astype(BF16), wg_ref[...], preferred_element_type=F32))
    x = x + gate * jnp.dot(p_ref[...].astype(BF16), wp_ref[...], preferred_element_type=F32)
    if final:
        ms = jnp.mean(x * x, axis=-1, keepdims=True)
        x = (x * lax.rsqrt(ms + EPS)) * fg_ref[...]
    out_ref[...] = x


def _post(o, x, p, layer, w_out, w_gate, w_proj, final_g):
    n = x.shape[0]
    rows = PRE_ROWS
    final = final_g is not None
    row_spec = lambda width: pl.BlockSpec((rows, width), lambda i: (i, 0))
    full = lambda shape: pl.BlockSpec(shape, lambda i: (0, 0))
    in_specs = [row_spec(WIDTH), row_spec(D_MODEL),
                pl.BlockSpec((None, rows, PLE_DIM), lambda i: (layer, i, 0)),
                full(w_out.shape), full(w_gate.shape), full(w_proj.shape)]
    args = [o, x, p, w_out, w_gate, w_proj]
    if final:
        in_specs.append(full((1, D_MODEL)))
        args.append(final_g.reshape(1, D_MODEL))
    return pl.pallas_call(
        functools.partial(_post_body, final=final),
        grid=(n // rows,),
        in_specs=in_specs,
        out_specs=row_spec(D_MODEL),
        out_shape=jax.ShapeDtypeStruct((n, D_MODEL), F32),
        compiler_params=_params(1),
        name="post_final" if final else "post",
    )(*args)


def _mixer_order(depth):
    return [("a", i // 2) if i % 2 == 0 else ("b", i // 2) for i in range(depth)]


def _lam_init(i):
    return 0.8 - 0.6 * math.exp(-0.3 * i)


def _trunk_prompt(x, p, pos, weights):
    (a_norm_g, a_w_in, a_lam, a_subln_g, a_w_out, b_norm_g, b_w_in, b_w_out,
     ple_w_proj, ple_w_gate, final_norm_g) = weights
    b, t, _ = x.shape
    n = b * t
    depth = p.shape[0]
    xf = x.reshape(n, D_MODEL)
    tables = _rope_tables(pos, PRE_ROWS)
    tri = jnp.tri(TRI, dtype=BF16)
    shp = (b, t, WIDTH)
    ak, av, bk, bv = [], [], [], []
    for i, (kind, j) in enumerate(_mixer_order(depth)):
        if kind == "a":
            q, kt, ktb, v, vb, gate = _pre_cols(xf, b, t, a_norm_g[j], a_w_in[j], tables, False)
            o = _attn_a_prompt(q.reshape(shp), ktb, vb.reshape(shp), gate.reshape(shp),
                               a_lam[j], a_subln_g[j].reshape(1, 2 * A_DH), _lam_init(i))
            ak.append(kt.reshape(b, A_HEADS, 2, A_DH, t).transpose(0, 4, 1, 2, 3))
            av.append(v.reshape(b, t, A_HEADS, 2 * A_DH))
            w_out = a_w_out[j]
        else:
            q, kt, ktb, vt, vtb, gate = _pre_cols(xf, b, t, b_norm_g[j], b_w_in[j], None, True)
            o = _attn_b_prompt(q.reshape(shp), ktb, vtb, gate.reshape(shp), tri)
            bk.append(kt.reshape(b, B_HEADS, B_DH, t).transpose(0, 3, 1, 2))
            bv.append(vt.reshape(b, B_HEADS, B_DH, t).transpose(0, 3, 1, 2))
            w_out = b_w_out[j]
        xf = _post(o.reshape(n, WIDTH), xf, p.reshape(depth, n, PLE_DIM), i, w_out,
                   ple_w_gate[i], ple_w_proj[i], final_norm_g if i == depth - 1 else None)
    return (xf.reshape(b, t, D_MODEL), jnp.stack(ak), jnp.stack(av), jnp.stack(bk), jnp.stack(bv))


def _trunk_sample(x, p, pos, weights, caches):
    (a_norm_g, a_w_in, a_lam, a_subln_g, a_w_out, b_norm_g, b_w_in, b_w_out,
     ple_w_proj, ple_w_gate, final_norm_g) = weights
    cache_a_k, cache_a_v, cache_b_k, cache_b_v = caches
    b, t, _ = x.shape
    n = b * t
    depth = p.shape[0]
    past = cache_a_k.shape[2]
    xf = x.reshape(n, D_MODEL)
    lane_tables, _ = _rope_tables(pos, PRE_ROWS)
    tri = jnp.tri(TRI, dtype=BF16)
    shp = (b, t, WIDTH)
    ak, av, bk, bv = [], [], [], []
    for i, (kind, j) in enumerate(_mixer_order(depth)):
        if kind == "a":
            q, k, kb, v, vb, gate = _pre_rows(xf, a_norm_g[j], a_w_in[j], lane_tables)
            kt_past = cache_a_k[j].transpose(0, 2, 3, 4, 1).reshape(b, WIDTH, past)
            v_past = cache_a_v[j].reshape(b, past * A_HEADS, 2 * A_DH)
            o = _attn_a_sample(q.reshape(shp), kb.reshape(shp), vb.reshape(shp), kt_past, v_past,
                               gate.reshape(shp), a_lam[j], a_subln_g[j].reshape(1, 2 * A_DH), _lam_init(i))
            ak.append(k.reshape(b, t, A_HEADS, 2, A_DH))
            av.append(v.reshape(b, t, A_HEADS, 2 * A_DH))
            w_out = a_w_out[j]
        else:
            q, k, kb, v, vb, gate = _pre_rows(xf, b_norm_g[j], b_w_in[j], None)
            kt_past = cache_b_k[j].transpose(0, 2, 3, 1).reshape(b, WIDTH, past)
            vt_past = cache_b_v[j].transpose(0, 2, 3, 1).reshape(b, WIDTH, past)
            o = _attn_b_sample(q.reshape(shp), kb.reshape(shp), vb.reshape(shp), kt_past, vt_past,
                               gate.reshape(shp), tri)
            bk.append(k.reshape(b, t, B_HEADS, B_DH))
            bv.append(v.reshape(b, t, B_HEADS, B_DH))
            w_out = b_w_out[j]
        xf = _post(o.reshape(n, WIDTH), xf, p.reshape(depth, n, PLE_DIM), i, w_out,
                   ple_w_gate[i], ple_w_proj[i], final_norm_g if i == depth - 1 else None)
    return (xf.reshape(b, t, D_MODEL), jnp.stack(ak), jnp.stack(av), jnp.stack(bk), jnp.stack(bv))


def kernel(x_prompt, x_sample, cache_a_k, cache_a_v, cache_b_k, cache_b_v, p_prompt, p_sample,
           a_norm_g, a_w_in, a_lam_q1, a_lam_k1, a_lam_q2, a_lam_k2, a_subln_g, a_w_out,
           b_norm_g, b_w_in, b_w_out, ple_w_proj, ple_w_gate, final_norm_g):
    past = cache_a_k.shape[2]
    a_lam = jnp.stack([a_lam_q1, a_lam_k1, a_lam_q2, a_lam_k2], axis=1)
    weights = (a_norm_g, a_w_in.astype(BF16), a_lam, a_subln_g, a_w_out.astype(BF16),
               b_norm_g, b_w_in.astype(BF16), b_w_out.astype(BF16),
               ple_w_proj.astype(BF16), ple_w_gate.astype(BF16), final_norm_g)
    pos_p = jnp.arange(x_prompt.shape[1], dtype=jnp.int32)
    pos_s = past + jnp.arange(x_sample.shape[1], dtype=jnp.int32)
    y_s, ak_s, av_s, bk_s, bv_s = _trunk_sample(x_sample, p_sample, pos_s, weights,
                                                (cache_a_k, cache_a_v, cache_b_k, cache_b_v))
    y_p, ak_p, av_p, bk_p, bv_p = _trunk_prompt(x_prompt, p_prompt, pos_p, weights)
    return (y_p, y_s, ak_p, av_p, bk_p, bv_p, ak_s, av_s, bk_s, bv_s)
```

```python
import functools
import math

import jax
import jax.numpy as jnp
from jax import lax
from jax.experimental import pallas as pl
from jax.experimental.pallas import tpu as pltpu

F32 = jnp.float32
BF16 = jnp.bfloat16

D_MODEL = 1024
CHUNK = 64
PLE_DIM = 256
ROPE_THETA = 500000.0
EPS = 1e-6
NEG_INF = -1e30
A_HEADS = 8
A_DH = 64
A_ROT = A_DH // 4
B_HEADS = 16
B_DH = 64
WIDTH = 1024
LANES = 128
SUBLANES = 8
HALF = 64
Q_SCALE = A_DH ** -0.5 * math.log2(math.e)

PRE_ROWS = 256
KV_TILE = 256
TRI = 256
A_BQ, A_BK = 2048, 512
B_BQ = 2048
A_SLAB = 512
STICK_EXIT = 151.0
SAMPLE_B_CHUNK = 256
SAMPLE_CHUNK = 2048
VMEM_LIMIT = 48 * 1024 * 1024

_NT = (((1,), (1,)), ((), ()))


def _params(n_axes, vmem_limit=VMEM_LIMIT):
    return pltpu.CompilerParams(
        dimension_semantics=("arbitrary",) * n_axes, vmem_limit_bytes=vmem_limit)


def _sigmoid(x):
    return 1.0 / (1.0 + jnp.exp(-x))


def _mod_pow2(x, n):
    assert n & (n - 1) == 0
    return x & (n - 1)


def _chunk_of(pos):
    assert CHUNK & (CHUNK - 1) == 0
    return pos >> (CHUNK.bit_length() - 1)


def _rope_lanes(sl, c_ref, s1_ref, s2_ref):
    return (sl * c_ref[...] + pltpu.roll(sl, LANES - A_ROT // 2, 1) * s1_ref[...]
            + pltpu.roll(sl, A_ROT // 2, 1) * s2_ref[...])


def _normed(x_ref, g_ref):
    x = x_ref[...]
    ms = jnp.mean(x * x, axis=-1, keepdims=True)
    return ((x * lax.rsqrt(ms + EPS)) * g_ref[...]).astype(BF16)


def _pre_rows_body(*refs, rope):
    if rope:
        x_ref, g_ref, wq_ref, wk_ref, wv_ref, wg_ref, c_ref, s1_ref, s2_ref = refs[:9]
        outs = refs[9:]
    else:
        x_ref, g_ref, wq_ref, wk_ref, wv_ref, wg_ref = refs[:6]
        outs = refs[6:]
    q_ref, k_ref, kb_ref, v_ref, vb_ref, gate_ref = outs
    hb = _normed(x_ref, g_ref)

    def rot(r, j):
        sl = r[:, j * LANES:(j + 1) * LANES]
        return _rope_lanes(sl, c_ref, s1_ref, s2_ref) if rope else sl

    q = jnp.dot(hb, wq_ref[...], preferred_element_type=F32)
    for j in range(WIDTH // LANES):
        q_ref[:, j * LANES:(j + 1) * LANES] = (rot(q, j) * Q_SCALE).astype(BF16)
    k = jnp.dot(hb, wk_ref[...], preferred_element_type=F32)
    for j in range(WIDTH // LANES):
        kr = rot(k, j)
        k_ref[:, j * LANES:(j + 1) * LANES] = kr
        kb_ref[:, j * LANES:(j + 1) * LANES] = kr.astype(BF16)
    v = jnp.dot(hb, wv_ref[...], preferred_element_type=F32)
    v_ref[...] = v
    vb_ref[...] = v.astype(BF16)
    gate_ref[...] = jnp.dot(hb, wg_ref[...], preferred_element_type=F32)


def _pre_cols_body(*refs, rope, v_cols):
    if rope:
        x_ref, g_ref, wq_ref, wkt_ref, wv_ref, wg_ref, c_ref, s1_ref, s2_ref, ct_ref, st_ref = refs[:11]
        outs = refs[11:]
    else:
        x_ref, g_ref, wq_ref, wkt_ref, wv_ref, wg_ref = refs[:6]
        outs = refs[6:]
    q_ref, kt_ref, ktb_ref, v_ref, vb_ref, gate_ref = outs
    hb = _normed(x_ref, g_ref)

    q = jnp.dot(hb, wq_ref[...], preferred_element_type=F32)
    for j in range(WIDTH // LANES):
        sl = q[:, j * LANES:(j + 1) * LANES]
        if rope:
            sl = _rope_lanes(sl, c_ref, s1_ref, s2_ref)
        q_ref[:, j * LANES:(j + 1) * LANES] = (sl * Q_SCALE).astype(BF16)

    kt = lax.dot_general(wkt_ref[...], hb, _NT, preferred_element_type=F32)
    if rope:
        half = A_ROT // 2
        assert half == SUBLANES
        cos, sin = ct_ref[...], st_ref[...]
        for grp in range(WIDTH // A_DH):
            r0 = grp * A_DH
            x0, x1 = kt[r0:r0 + half], kt[r0 + half:r0 + 2 * half]
            y0, y1 = x0 * cos - x1 * sin, x1 * cos + x0 * sin
            kt_ref[r0:r0 + half, :] = y0
            kt_ref[r0 + half:r0 + 2 * half, :] = y1
            kt_ref[r0 + 2 * half:r0 + A_DH, :] = kt[r0 + 2 * half:r0 + A_DH]
            ktb_ref[r0:r0 + A_DH, :] = jnp.concatenate(
                [y0, y1, kt[r0 + 2 * half:r0 + A_DH]], axis=0).astype(BF16)
    else:
        kt_ref[...] = kt
        ktb_ref[...] = kt.astype(BF16)

    if v_cols:
        vt = lax.dot_general(wv_ref[...], hb, _NT, preferred_element_type=F32)
        v_ref[...] = vt
        vb_ref[...] = vt.astype(BF16)
    else:
        v = jnp.dot(hb, wv_ref[...], preferred_element_type=F32)
        v_ref[...] = v
        vb_ref[...] = v.astype(BF16)
    gate_ref[...] = jnp.dot(hb, wg_ref[...], preferred_element_type=F32)


def _split_w(w_in, t_k, t_v):
    parts = [w_in[:, i * WIDTH:(i + 1) * WIDTH] for i in range(4)]
    if t_k:
        parts[1] = parts[1].T
    if t_v:
        parts[2] = parts[2].T
    return parts


def _pre_rows(x, norm_g, w_in, tables):
    n = x.shape[0]
    rows = PRE_ROWS
    rope = tables is not None
    row_spec = lambda width: pl.BlockSpec((rows, width), lambda i: (i, 0))
    full = lambda shape: pl.BlockSpec(shape, lambda i: (0, 0))
    in_specs = [row_spec(D_MODEL), full((1, D_MODEL))] + [full((D_MODEL, WIDTH))] * 4
    args = [x, norm_g.reshape(1, D_MODEL)] + _split_w(w_in, False, False)
    if rope:
        in_specs += [full((rows, LANES))] * 3
        args += list(tables)
    out_shape = [jax.ShapeDtypeStruct((n, WIDTH), dt) for dt in (BF16, F32, BF16, F32, BF16, F32)]
    return pl.pallas_call(
        functools.partial(_pre_rows_body, rope=rope),
        grid=(n // rows,),
        in_specs=in_specs,
        out_specs=[row_spec(WIDTH)] * 6,
        out_shape=out_shape,
        compiler_params=_params(1),
        name="pre_rows_rope" if rope else "pre_rows",
    )(*args)


def _pre_cols(x, b, t, norm_g, w_in, tables, v_cols):
    rows = PRE_ROWS
    assert rows == KV_TILE and t % rows == 0
    n_t = t // rows
    rope = tables is not None
    row_spec = lambda width: pl.BlockSpec((rows, width), lambda i: (i, 0))
    full = lambda shape: pl.BlockSpec(shape, lambda i: (0, 0))
    colf_spec = pl.BlockSpec((None, WIDTH, rows), lambda i: (i // n_t, 0, i % n_t))
    colb_spec = pl.BlockSpec((None, None, WIDTH, rows), lambda i: (i // n_t, i % n_t, 0, 0))
    in_specs = [row_spec(D_MODEL), full((1, D_MODEL))] + [full((D_MODEL, WIDTH))] * 4
    args = [x, norm_g.reshape(1, D_MODEL)] + _split_w(w_in, True, v_cols)
    if rope:
        lanes_tab, rows_tab = tables
        in_specs += [pl.BlockSpec((rows, LANES), lambda i: (i % n_t, 0))] * 3
        in_specs += [pl.BlockSpec((A_ROT // 2, rows), lambda i: (0, i % n_t))] * 2
        args += list(lanes_tab) + list(rows_tab)
    colf = jax.ShapeDtypeStruct((b, WIDTH, t), F32)
    colb = jax.ShapeDtypeStruct((b, n_t, WIDTH, rows), BF16)
    rowf = jax.ShapeDtypeStruct((b * t, WIDTH), F32)
    rowb = jax.ShapeDtypeStruct((b * t, WIDTH), BF16)
    out_shape = [rowb, colf, colb] + ([colf, colb] if v_cols else [rowf, rowb]) + [rowf]
    out_specs = ([row_spec(WIDTH), colf_spec, colb_spec]
                 + ([colf_spec, colb_spec] if v_cols else [row_spec(WIDTH)] * 2) + [row_spec(WIDTH)])
    return pl.pallas_call(
        functools.partial(_pre_cols_body, rope=rope, v_cols=v_cols),
        grid=(b * n_t,),
        in_specs=in_specs,
        out_specs=out_specs,
        out_shape=out_shape,
        compiler_params=_params(1),
        name="pre_cols_rope" if rope else "pre_cols",
    )(*args)


def _rope_tables(pos, rows):
    half = A_ROT // 2
    inv = ROPE_THETA ** (-jnp.arange(half, dtype=F32) * 2.0 / A_ROT)
    ang = pos.astype(F32)[:, None] * inv[None, :]
    cos, sin = jnp.cos(ang), jnp.sin(ang)
    t = pos.shape[0]
    c = jnp.concatenate([cos, cos, jnp.ones((t, HALF - A_ROT), F32)], axis=1)
    s1 = jnp.concatenate([-sin, jnp.zeros((t, HALF - half), F32)], axis=1)
    s2 = jnp.concatenate([jnp.zeros((t, half), F32), sin, jnp.zeros((t, HALF - A_ROT), F32)], axis=1)
    reps = (max(rows // t, 1), LANES // HALF)
    return tuple(jnp.tile(a, reps) for a in (c, s1, s2)), (cos.T, sin.T)


def _lo_lanes():
    return lax.broadcasted_iota(jnp.int32, (1, LANES), 1) < HALF


def _halves(q):
    lo = _lo_lanes()
    zero = jnp.zeros_like(q)
    return jnp.where(lo, q, zero), jnp.where(lo, zero, q)


def _lam(lam_ref, lam_init):
    a = jnp.sum(lam_ref[0:1, :] * lam_ref[1:2, :], axis=-1, keepdims=True)
    b = jnp.sum(lam_ref[2:3, :] * lam_ref[3:4, :], axis=-1, keepdims=True)
    return jnp.exp(a) - jnp.exp(b) + lam_init


def _diff_epilogue(o1, o2, lam_ref, subg_ref, g_ref, o_ref, lam_init):
    o = o1 - _lam(lam_ref, lam_init) * o2
    ms = jnp.mean(o * o, axis=-1, keepdims=True)
    o = (o * lax.rsqrt(ms + EPS)) * subg_ref[...] * (1.0 - lam_init)
    g = g_ref[...]
    o_ref[...] = (o * (g * _sigmoid(g))).astype(o_ref.dtype)


def _softplus2(z):
    sign_bit = jnp.uint32(1 << 31)
    neg_abs = lax.bitcast_convert_type(lax.bitcast_convert_type(z, jnp.uint32) | sign_bit, F32)
    return jnp.maximum(z, 0.0) + jnp.log2(1.0 + jnp.exp2(neg_abs))


def _stick_weights(z, valid, carry, tri):
    sp = _softplus2(z)
    if valid is not None:
        sp = jnp.where(valid, sp, 0.0)
    pieces = [None] * (z.shape[1] // TRI)
    for sb in range(len(pieces) - 1, -1, -1):
        cols = slice(sb * TRI, (sb + 1) * TRI)
        sps = sp[:, cols]
        suffix = jnp.dot(sps.astype(BF16), tri, preferred_element_type=F32)
        pieces[sb] = jnp.exp2(jnp.minimum(z[:, cols] - suffix, 0.0) - carry)
        carry = carry + suffix[:, :1]
    a = pieces[0] if len(pieces) == 1 else jnp.concatenate(pieces, axis=1)
    if valid is not None:
        a = jnp.where(valid, a, 0.0)
    return a, carry


def _kv_tiles(ref, kj, bk):
    per = bk // KV_TILE
    tiles = [ref[kj * per + i] for i in range(per)]
    return tiles[0] if per == 1 else jnp.concatenate(tiles, axis=1)


def _attn_a_prompt_body(lam_ref, subg_ref, q_ref, kt_ref, v_ref, g_ref, o_ref,
                        qz_ref, m_ref, l_ref, acc_ref, *, bq, bk, diag_slab, lam_init):
    qi = pl.program_id(2)
    q1, q2 = _halves(q_ref[...])
    qz_ref[0] = q1
    qz_ref[1] = q2
    m_ref[...] = jnp.full(m_ref.shape, NEG_INF, F32)
    l_ref[...] = jnp.zeros(l_ref.shape, F32)
    acc_ref[...] = jnp.zeros(acc_ref.shape, F32)

    def step(kj, diag):
        kt = _kv_tiles(kt_ref, kj, bk)
        v = v_ref[pl.ds(pl.multiple_of(kj * bk, bk), bk), :]
        if diag is not None:
            v = jnp.concatenate([v, jnp.ones_like(v)], axis=1)
        slab = bq if diag is None else diag_slab
        for r0 in range(0, bq, slab):
            if diag is None or _chunk_of((diag + 1) * bk - 1) <= _chunk_of(r0):
                vis = None
            elif _chunk_of(diag * bk) > _chunk_of(r0 + slab - 1):
                continue
            else:
                qpos = r0 + lax.broadcasted_iota(jnp.int32, (slab, bk), 0)
                kpos = diag * bk + lax.broadcasted_iota(jnp.int32, (slab, bk), 1)
                vis = _chunk_of(kpos) <= _chunk_of(qpos)
            rows = slice(r0, r0 + slab)
            for c in range(2):
                s = jnp.dot(qz_ref[c, rows, :], kt, preferred_element_type=F32)
                if vis is not None:
                    s = jnp.where(vis, s, NEG_INF)
                m_prev = m_ref[c, rows, :]
                m_new = jnp.maximum(m_prev, jnp.max(s, axis=-1, keepdims=True))
                alpha = jnp.exp2(m_prev - m_new)
                p = jnp.exp2(s - jnp.tile(m_new, (1, bk // LANES)))
                pv = jnp.dot(p.astype(BF16), v, preferred_element_type=F32)
                psum = jnp.sum(p, axis=-1, keepdims=True) if diag is None else pv[:, LANES:]
                l_ref[c, rows, :] = alpha * l_ref[c, rows, :] + psum
                acc_ref[c, rows, :] = alpha * acc_ref[c, rows, :] + pv[:, :LANES]
                m_ref[c, rows, :] = m_new

    n_full = qi * (bq // bk)

    def body(kj, c):
        step(kj, None)
        return c

    lax.fori_loop(0, n_full, body, 0)
    for r in range(bq // bk):
        step(n_full + r, r)

    _diff_epilogue(acc_ref[0] / l_ref[0], acc_ref[1] / l_ref[1],
                   lam_ref, subg_ref, g_ref, o_ref, lam_init)


def _attn_a_prompt(q, ktb, vb, gate, lam4, subg, lam_init):
    b, t, _ = q.shape
    bq, bk = A_BQ, A_BK
    assert bq % bk == 0 and bk % KV_TILE == 0 and t % bq == 0 and bk % CHUNK == 0
    q_spec = pl.BlockSpec((None, bq, LANES), lambda bi, h, qi: (bi, qi, h))
    kt_spec = pl.BlockSpec((None, t // KV_TILE, LANES, KV_TILE), lambda bi, h, qi: (bi, 0, h, 0))
    v_spec = pl.BlockSpec((None, t, LANES), lambda bi, h, qi: (bi, 0, h))
    full = lambda shape: pl.BlockSpec(shape, lambda bi, h, qi: (0,) * len(shape))
    return pl.pallas_call(
        functools.partial(_attn_a_prompt_body, bq=bq, bk=bk, diag_slab=A_SLAB, lam_init=lam_init),
        grid=(b, A_HEADS, t // bq),
        in_specs=[full(lam4.shape), full(subg.shape), q_spec, kt_spec, v_spec, q_spec],
        out_specs=q_spec,
        out_shape=jax.ShapeDtypeStruct((b, t, WIDTH), BF16),
        scratch_shapes=[pltpu.VMEM((2, bq, LANES), BF16),
                        pltpu.VMEM((2, bq, LANES), F32),
                        pltpu.VMEM((2, bq, LANES), F32),
                        pltpu.VMEM((2, bq, LANES), F32)],
        compiler_params=_params(3),
        name="attn_a_prompt",
    )(lam4, subg, q, ktb, vb, gate)


def _attn_b_prompt_body(tri_ref, q_ref, kt_ref, vt_ref, g_ref, o_ref,
                        qz_ref, carry_ref, acc_ref, *, bq, sub):
    qi = pl.program_id(2)
    q1, q2 = _halves(q_ref[...])
    qz_ref[0] = q1
    qz_ref[1] = q2
    carry_ref[...] = jnp.zeros(carry_ref.shape, F32)
    acc_ref[...] = jnp.zeros(acc_ref.shape, F32)

    def fold(rows, kt, vt, valid):
        n_sb = kt.shape[1] // TRI
        zs = [jnp.dot(qz_ref[c, rows, :], kt, preferred_element_type=F32) for c in range(2)]
        sps = [_softplus2(z) for z in zs]
        if valid is not None:
            sps = [jnp.where(valid, sp, 0.0) for sp in sps]
        carries = [carry_ref[c, rows, :] for c in range(2)]
        pieces = [[None] * n_sb for _ in range(2)]
        for sb in range(n_sb - 1, -1, -1):
            cols = slice(sb * TRI, (sb + 1) * TRI)
            sfx = [jnp.dot(sp[:, cols].astype(BF16), tri_ref[...], preferred_element_type=F32) for sp in sps]
            for c in range(2):
                pieces[c][sb] = jnp.exp2(jnp.minimum(zs[c][:, cols] - sfx[c], 0.0) - carries[c])
                carries[c] = carries[c] + sfx[c][:, :1]
        for c in range(2):
            a = pieces[c][0] if n_sb == 1 else jnp.concatenate(pieces[c], axis=1)
            if valid is not None:
                a = jnp.where(valid, a, 0.0)
            acc_ref[c, rows, :] += lax.dot_general(a.astype(BF16), vt, _NT, preferred_element_type=F32)
            carry_ref[c, rows, :] = carries[c]

    assert sub == KV_TILE
    row = lax.broadcasted_iota(jnp.int32, (sub, 2 * sub), 0)
    col = lax.broadcasted_iota(jnp.int32, (sub, 2 * sub), 1)
    causal = col < row + sub
    for j in range(bq // sub):
        rows = slice(j * sub, (j + 1) * sub)
        own = qi * (bq // sub) + j
        prev = jnp.maximum(own - 1, 0)
        kt = jnp.concatenate([kt_ref[prev], kt_ref[own]], axis=1)
        vt = jnp.concatenate([vt_ref[prev], vt_ref[own]], axis=1)
        valid = jnp.logical_and(causal, jnp.logical_or(col >= sub, own > 0))
        fold(rows, kt, vt, valid)

    for j in range(bq // sub):
        rows = slice(j * sub, (j + 1) * sub)
        own = qi * (bq // sub) + j

        def exhausted(rows=rows):
            return jnp.min(carry_ref[:, rows, :]) >= STICK_EXIT

        def cond(state):
            t, done = state
            return jnp.logical_and(t >= 0, jnp.logical_not(done))

        def body(state, rows=rows, exhausted=exhausted):
            t, _ = state
            fold(rows, kt_ref[t], vt_ref[t], None)
            return t - 1, exhausted()

        lax.while_loop(cond, body, (own - 2, exhausted()))

    g = g_ref[...]
    o = jnp.where(_lo_lanes(), acc_ref[0], acc_ref[1])
    o_ref[...] = (o * (g * _sigmoid(g))).astype(o_ref.dtype)


def _attn_b_prompt(q, ktb, vtb, gate, tri):
    b, t, _ = q.shape
    bq = B_BQ
    assert bq % KV_TILE == 0 and t % bq == 0 and KV_TILE % TRI == 0
    q_spec = pl.BlockSpec((None, bq, LANES), lambda bi, h, qi: (bi, qi, h))
    kt_spec = pl.BlockSpec((None, t // KV_TILE, LANES, KV_TILE), lambda bi, h, qi: (bi, 0, h, 0))
    return pl.pallas_call(
        functools.partial(_attn_b_prompt_body, bq=bq, sub=KV_TILE),
        grid=(b, WIDTH // LANES, t // bq),
        in_specs=[pl.BlockSpec((TRI, TRI), lambda bi, h, qi: (0, 0)),
                  q_spec, kt_spec, kt_spec, q_spec],
        out_specs=q_spec,
        out_shape=jax.ShapeDtypeStruct((b, t, WIDTH), BF16),
        scratch_shapes=[pltpu.VMEM((2, bq, LANES), BF16),
                        pltpu.VMEM((2, bq, 1), F32),
                        pltpu.VMEM((2, bq, LANES), F32)],
        compiler_params=_params(3),
        name="attn_b_prompt",
    )(tri, q, ktb, vtb, gate)


def _pad_rows(x, rows):
    return jnp.concatenate([x, jnp.zeros((rows - x.shape[0], x.shape[1]), x.dtype)], axis=0)


def _attn_a_sample_body(lam_ref, subg_ref, q_ref, kn_ref, vn_ref, ktp_ref, vp_ref, g_ref, o_ref,
                        qq_ref, m_ref, l_ref, acc_ref, *, t_new, past, chunk, lam_init):
    pc = pl.program_id(1)

    @pl.when(pc == 0)
    def _():
        for h in range(A_HEADS):
            qq_ref[h] = jnp.concatenate(_halves(q_ref[:, h * LANES:(h + 1) * LANES]), axis=0)
        m_ref[...] = jnp.full(m_ref.shape, NEG_INF, F32)
        l_ref[...] = jnp.zeros(l_ref.shape, F32)
        acc_ref[...] = jnp.zeros(acc_ref.shape, F32)

    def update(h, s, v):
        m_prev = m_ref[h]
        m_new = jnp.maximum(m_prev, jnp.max(s, axis=-1, keepdims=True))
        alpha = jnp.exp2(m_prev - m_new)
        p = jnp.exp2(s - jnp.tile(m_new, (1, s.shape[1] // LANES)))
        l_ref[h] = alpha * l_ref[h] + jnp.sum(p, axis=-1, keepdims=True)
        acc_ref[h] = alpha * acc_ref[h] + jnp.dot(p.astype(BF16), v, preferred_element_type=F32)
        m_ref[h] = m_new

    for h in range(A_HEADS):
        kt = ktp_ref[h * LANES:(h + 1) * LANES, :].astype(BF16)
        v = vp_ref[pl.ds(h, chunk, stride=A_HEADS), :].astype(BF16)
        update(h, jnp.dot(qq_ref[h], kt, preferred_element_type=F32), v)

    @pl.when(pc == pl.num_programs(1) - 1)
    def _():
        row = lax.broadcasted_iota(jnp.int32, (2 * t_new, LANES), 0)
        col = lax.broadcasted_iota(jnp.int32, (2 * t_new, LANES), 1)
        vis = jnp.logical_and(col < t_new,
                              _chunk_of(past + col) <= _chunk_of(past + _mod_pow2(row, t_new)))
        for h in range(A_HEADS):
            cols = slice(h * LANES, (h + 1) * LANES)
            kn = _pad_rows(kn_ref[:, cols], LANES)
            vn = _pad_rows(vn_ref[:, cols], LANES)
            s_n = lax.dot_general(qq_ref[h], kn, _NT, preferred_element_type=F32)
            update(h, jnp.where(vis, s_n, NEG_INF), vn)
            o = acc_ref[h] / l_ref[h]
            _diff_epilogue(o[:t_new], o[t_new:], lam_ref, subg_ref, g_ref.at[:, cols], o_ref.at[:, cols],
                           lam_init)


def _attn_a_sample(q, kb, vb, kt_past, v_past, gate, lam4, subg, lam_init):
    b, t_new, _ = q.shape
    past = kt_past.shape[2]
    chunk = SAMPLE_CHUNK
    assert v_past.shape[1] == past * A_HEADS and past % chunk == 0
    new_spec = pl.BlockSpec((None, t_new, WIDTH), lambda bi, pc: (bi, 0, 0))
    ktp_spec = pl.BlockSpec((None, WIDTH, chunk), lambda bi, pc: (bi, 0, pc))
    vp_spec = pl.BlockSpec((None, chunk * A_HEADS, LANES), lambda bi, pc: (bi, pc, 0))
    full = lambda shape: pl.BlockSpec(shape, lambda bi, pc: (0,) * len(shape))
    state = pltpu.VMEM((A_HEADS, 2 * t_new, LANES), F32)
    return pl.pallas_call(
        functools.partial(_attn_a_sample_body, t_new=t_new, past=past, chunk=chunk, lam_init=lam_init),
        grid=(b, past // chunk),
        in_specs=[full(lam4.shape), full(subg.shape), new_spec, new_spec, new_spec,
                  ktp_spec, vp_spec, new_spec],
        out_specs=new_spec,
        out_shape=jax.ShapeDtypeStruct((b, t_new, WIDTH), BF16),
        scratch_shapes=[pltpu.VMEM((A_HEADS, 2 * t_new, LANES), BF16), state, state, state],
        compiler_params=_params(2),
        name="attn_a_sample",
    )(lam4, subg, q, kb, vb, kt_past, v_past, gate)


def _attn_b_sample_body(tri_ref, q_ref, kn_ref, vn_ref, g_ref, kt_hbm, vt_hbm, o_ref,
                        kbuf, vbuf, sems, qq_ref, carry_ref, acc_ref, *, t_new, past, chunk):
    bi = pl.program_id(0)
    rows = 2 * t_new
    n_blk = chunk // TRI
    n_chunks = past // chunk
    n_pairs = WIDTH // LANES
    slot = lax.rem(bi, 2)
    tri = tri_ref[...]

    def fetch(batch, ci, into):
        cols = pl.ds(pl.multiple_of(ci * chunk, chunk), chunk)
        return (pltpu.make_async_copy(kt_hbm.at[batch, :, cols], kbuf.at[into], sems.at[0, into]),
                pltpu.make_async_copy(vt_hbm.at[batch, :, cols], vbuf.at[into], sems.at[1, into]))

    @pl.when(bi == 0)
    def _():
        for cp in fetch(0, n_chunks - 1, 0):
            cp.start()

    @pl.when(bi + 1 < pl.num_programs(0))
    def _():
        for cp in fetch(bi + 1, n_chunks - 1, 1 - slot):
            cp.start()

    row = lax.broadcasted_iota(jnp.int32, (rows, TRI), 0)
    col = lax.broadcasted_iota(jnp.int32, (rows, TRI), 1)
    valid = col < _mod_pow2(row, t_new)
    for hp in range(n_pairs):
        cols = slice(hp * LANES, (hp + 1) * LANES)
        qq = jnp.concatenate(_halves(q_ref[:, cols]), axis=0)
        qq_ref[hp] = qq
        kn = _pad_rows(kn_ref[:, cols], TRI)
        vn = _pad_rows(vn_ref[:, cols], TRI)
        z_n = lax.dot_general(qq, kn, _NT, preferred_element_type=F32)
        a_n, carry = _stick_weights(z_n, valid, jnp.zeros((rows, 1), F32), tri)
        acc_ref[hp] = jnp.dot(a_n.astype(BF16), vn, preferred_element_type=F32)
        carry_ref[hp] = carry

    def absorb():
        for hp in range(n_pairs):
            feat = pl.ds(hp * LANES, LANES)
            z_all = jnp.dot(qq_ref[hp], kbuf[slot, feat, :].astype(BF16), preferred_element_type=F32)
            z = jnp.concatenate([z_all[:, c * TRI:(c + 1) * TRI] for c in range(n_blk)], axis=0)
            suffix = jnp.dot(_softplus2(z).astype(BF16), tri, preferred_element_type=F32)
            running = carry_ref[hp]
            carries = [None] * n_blk
            for c in range(n_blk - 1, -1, -1):
                carries[c] = running
                running = running + suffix[c * rows:(c + 1) * rows, :1]
            carry_ref[hp] = running
            a = jnp.exp2(jnp.minimum(z - suffix, 0.0) - jnp.concatenate(carries, axis=0)).astype(BF16)
            vtp = vbuf[slot, feat, :].astype(BF16)
            acc = acc_ref[hp]
            for c in range(n_blk):
                acc = acc + lax.dot_general(a[c * rows:(c + 1) * rows], vtp[:, c * TRI:(c + 1) * TRI], _NT,
                                            preferred_element_type=F32)
            acc_ref[hp] = acc

    def exhausted():
        return jnp.min(carry_ref[...]) >= STICK_EXIT

    for cp in fetch(bi, n_chunks - 1, slot):
        cp.wait()
    absorb()

    def cond(state):
        ci, done = state
        return jnp.logical_and(ci >= 0, jnp.logical_not(done))

    def body(state):
        ci, _ = state
        copies = fetch(bi, ci, slot)
        for cp in copies:
            cp.start()
        for cp in copies:
            cp.wait()
        absorb()
        return ci - 1, exhausted()

    lax.while_loop(cond, body, (jnp.int32(n_chunks - 2), exhausted()))

    for hp in range(n_pairs):
        cols = slice(hp * LANES, (hp + 1) * LANES)
        acc = acc_ref[hp]
        g = g_ref[:, cols]
        o = jnp.where(_lo_lanes(), acc[:t_new], acc[t_new:])
        o_ref[:, cols] = (o * (g * _sigmoid(g))).astype(o_ref.dtype)


def _attn_b_sample(q, kb, vb, kt_past, vt_past, gate, tri):
    b, t_new, _ = q.shape
    past = kt_past.shape[2]
    chunk = SAMPLE_B_CHUNK
    assert past % chunk == 0 and chunk % TRI == 0 and t_new <= TRI
    n_pairs = WIDTH // LANES
    new_spec = pl.BlockSpec((None, t_new, WIDTH), lambda bi: (bi, 0, 0))
    hbm_spec = pl.BlockSpec(memory_space=pl.ANY)
    return pl.pallas_call(
        functools.partial(_attn_b_sample_body, t_new=t_new, past=past, chunk=chunk),
        grid=(b,),
        in_specs=[pl.BlockSpec((TRI, TRI), lambda bi: (0, 0)),
                  new_spec, new_spec, new_spec, new_spec, hbm_spec, hbm_spec],
        out_specs=new_spec,
        out_shape=jax.ShapeDtypeStruct((b, t_new, WIDTH), BF16),
        scratch_shapes=[pltpu.VMEM((2, WIDTH, chunk), F32),
                        pltpu.VMEM((2, WIDTH, chunk), F32),
                        pltpu.SemaphoreType.DMA((2, 2)),
                        pltpu.VMEM((n_pairs, 2 * t_new, LANES), BF16),
                        pltpu.VMEM((n_pairs, 2 * t_new, 1), F32),
                        pltpu.VMEM((n_pairs, 2 * t_new, LANES), F32)],
        compiler_params=_params(1),
        name="attn_b_sample",
    )(tri, q, kb, vb, gate, kt_past, vt_past)


def _post_body(*refs, final):
    if final:
        o_ref, x_ref, p_ref, wo_ref, wg_ref, wp_ref, fg_ref, out_ref = refs
    else:
        o_ref, x_ref, p_ref, wo_ref, wg_ref, wp_ref, out_ref = refs
    x = x_ref[...] + jnp.dot(o_ref[...], wo_ref[...], preferred_element_type=F32)
    gate = _sigmoid(jnp.dot(x.astype(BF16), wg_ref[...], preferred_element_type=F32))
    x = x + gate * jnp.dot(p_ref[...].astype(BF16), wp_ref[...], preferred_element_type=F32)
    if final:
        ms = jnp.mean(x * x, axis=-1, keepdims=True)
        x = (x * lax.rsqrt(ms + EPS)) * fg_ref[...]
    out_ref[...] = x


def _post(o, x, p, layer, w_out, w_gate, w_proj, final_g):
    n = x.shape[0]
    rows = PRE_ROWS
    final = final_g is not None
    row_spec = lambda width: pl.BlockSpec((rows, width), lambda i: (i, 0))
    full = lambda shape: pl.BlockSpec(shape, lambda i: (0, 0))
    in_specs = [row_spec(WIDTH), row_spec(D_MODEL),
                pl.BlockSpec((None, rows, PLE_DIM), lambda i: (layer, i, 0)),
                full(w_out.shape), full(w_gate.shape), full(w_proj.shape)]
    args = [o, x, p, w_out, w_gate, w_proj]
    if final:
        in_specs.append(full((1, D_MODEL)))
        args.append(final_g.reshape(1, D_MODEL))
    return pl.pallas_call(
        functools.partial(_post_body, final=final),
        grid=(n // rows,),
        in_specs=in_specs,
        out_specs=row_spec(D_MODEL),
        out_shape=jax.ShapeDtypeStruct((n, D_MODEL), F32),
        compiler_params=_params(1),
        name="post_final" if final else "post",
    )(*args)


def _mixer_order(depth):
    return [("a", i // 2) if i % 2 == 0 else ("b", i // 2) for i in range(depth)]


def _lam_init(i):
    return 0.8 - 0.6 * math.exp(-0.3 * i)


def _trunk_prompt(x, p, pos, weights):
    (a_norm_g, a_w_in, a_lam, a_subln_g, a_w_out, b_norm_g, b_w_in, b_w_out,
     ple_w_proj, ple_w_gate, final_norm_g) = weights
    b, t, _ = x.shape
    n = b * t
    depth = p.shape[0]
    xf = x.reshape(n, D_MODEL)
    tables = _rope_tables(pos, PRE_ROWS)
    tri = jnp.tri(TRI, dtype=BF16)
    shp = (b, t, WIDTH)
    ak, av, bk, bv = [], [], [], []
    for i, (kind, j) in enumerate(_mixer_order(depth)):
        if kind == "a":
            q, kt, ktb, v, vb, gate = _pre_cols(xf, b, t, a_norm_g[j], a_w_in[j], tables, False)
            o = _attn_a_prompt(q.reshape(shp), ktb, vb.reshape(shp), gate.reshape(shp),
                               a_lam[j], a_subln_g[j].reshape(1, 2 * A_DH), _lam_init(i))
            ak.append(kt.reshape(b, A_HEADS, 2, A_DH, t).transpose(0, 4, 1, 2, 3))
            av.append(v.reshape(b, t, A_HEADS, 2 * A_DH))
            w_out = a_w_out[j]
        else:
            q, kt, ktb, vt, vtb, gate = _pre_cols(xf, b, t, b_norm_g[j], b_w_in[j], None, True)
            o = _attn_b_prompt(q.reshape(shp), ktb, vtb, gate.reshape(shp), tri)
            bk.append(kt.reshape(b, B_HEADS, B_DH, t).transpose(0, 3, 1, 2))
            bv.append(vt.reshape(b, B_HEADS, B_DH, t).transpose(0, 3, 1, 2))
            w_out = b_w_out[j]
        xf = _post(o.reshape(n, WIDTH), xf, p.reshape(depth, n, PLE_DIM), i, w_out,
                   ple_w_gate[i], ple_w_proj[i], final_norm_g if i == depth - 1 else None)
    return (xf.reshape(b, t, D_MODEL), jnp.stack(ak), jnp.stack(av), jnp.stack(bk), jnp.stack(bv))


def _trunk_sample(x, p, pos, weights, caches):
    (a_norm_g, a_w_in, a_lam, a_subln_g, a_w_out, b_norm_g, b_w_in, b_w_out,
     ple_w_proj, ple_w_gate, final_norm_g) = weights
    cache_a_k, cache_a_v, cache_b_k, cache_b_v = caches
    b, t, _ = x.shape
    n = b * t
    depth = p.shape[0]
    past = cache_a_k.shape[2]
    xf = x.reshape(n, D_MODEL)
    lane_tables, _ = _rope_tables(pos, PRE_ROWS)
    tri = jnp.tri(TRI, dtype=BF16)
    shp = (b, t, WIDTH)
    ak, av, bk, bv = [], [], [], []
    for i, (kind, j) in enumerate(_mixer_order(depth)):
        if kind == "a":
            q, k, kb, v, vb, gate = _pre_rows(xf, a_norm_g[j], a_w_in[j], lane_tables)
            kt_past = cache_a_k[j].transpose(0, 2, 3, 4, 1).reshape(b, WIDTH, past)
            v_past = cache_a_v[j].reshape(b, past * A_HEADS, 2 * A_DH)
            o = _attn_a_sample(q.reshape(shp), kb.reshape(shp), vb.reshape(shp), kt_past, v_past,
                               gate.reshape(shp), a_lam[j], a_subln_g[j].reshape(1, 2 * A_DH), _lam_init(i))
            ak.append(k.reshape(b, t, A_HEADS, 2, A_DH))
            av.append(v.reshape(b, t, A_HEADS, 2 * A_DH))
            w_out = a_w_out[j]
        else:
            q, k, kb, v, vb, gate = _pre_rows(xf, b_norm_g[j], b_w_in[j], None)
            kt_past = cache_b_k[j].transpose(0, 2, 3, 1).reshape(b, WIDTH, past)
            vt_past = cache_b_v[j].transpose(0, 2, 3, 1).reshape(b, WIDTH, past)
            o = _attn_b_sample(q.reshape(shp), kb.reshape(shp), vb.reshape(shp), kt_past, vt_past,
                               gate.reshape(shp), tri)
            bk.append(k.reshape(b, t, B_HEADS, B_DH))
            bv.append(v.reshape(b, t, B_HEADS, B_DH))
            w_out = b_w_out[j]
        xf = _post(o.reshape(n, WIDTH), xf, p.reshape(depth, n, PLE_DIM), i, w_out,
                   ple_w_gate[i], ple_w_proj[i], final_norm_g if i == depth - 1 else None)
    return (xf.reshape(b, t, D_MODEL), jnp.stack(ak), jnp.stack(av), jnp.stack(bk), jnp.stack(bv))


def kernel(x_prompt, x_sample, cache_a_k, cache_a_v, cache_b_k, cache_b_v, p_prompt, p_sample,
           a_norm_g, a_w_in, a_lam_q1, a_lam_k1, a_lam_q2, a_lam_k2, a_subln_g, a_w_out,
           b_norm_g, b_w_in, b_w_out, ple_w_proj, ple_w_gate, final_norm_g):
    past = cache_a_k.shape[2]
    a_lam = jnp.stack([a_lam_q1, a_lam_k1, a_lam_q2, a_lam_k2], axis=1)
    weights = (a_norm_g, a_w_in.astype(BF16), a_lam, a_subln_g, a_w_out.astype(BF16),
               b_norm_g, b_w_in.astype(BF16), b_w_out.astype(BF16),
               ple_w_proj.astype(BF16), ple_w_gate.astype(BF16), final_norm_g)
    pos_p = jnp.arange(x_prompt.shape[1], dtype=jnp.int32)
    pos_s = past + jnp.arange(x_sample.shape[1], dtype=jnp.int32)
    y_s, ak_s, av_s, bk_s, bv_s = _trunk_sample(x_sample, p_sample, pos_s, weights,
                                                (cache_a_k, cache_a_v, cache_b_k, cache_b_v))
    y_p, ak_p, av_p, bk_p, bv_p = _trunk_prompt(x_prompt, p_prompt, pos_p, weights)
    return (y_p, y_s, ak_p, av_p, bk_p, bv_p, ak_s, av_s, bk_s, bv_s)
```

```python
import functools
import math

import jax
import jax.numpy as jnp
from jax import lax
from jax.experimental import pallas as pl
from jax.experimental.pallas import tpu as pltpu

F32 = jnp.float32
BF16 = jnp.bfloat16

D_MODEL = 1024
CHUNK = 64
PLE_DIM = 256
ROPE_THETA = 500000.0
EPS = 1e-6
NEG_INF = -1e30
A_HEADS = 8
A_DH = 64
A_ROT = A_DH // 4
B_HEADS = 16
B_DH = 64
WIDTH = 1024
LANES = 128
SUBLANES = 8
HALF = 64
Q_SCALE = A_DH ** -0.5 * math.log2(math.e)

PRE_ROWS = 256
PRE_COLS_ROWS = 512
POST_ROWS = 512
KV_TILE = 256
TRI = 256
A_BQ, A_BK = 2048, 512
B_BQ = 2048
A_SLAB = 512
STICK_EXIT = 151.0
SAMPLE_B_CHUNK = 256
SAMPLE_CHUNK = 2048
VMEM_LIMIT = 48 * 1024 * 1024

_NT = (((1,), (1,)), ((), ()))


def _params(n_axes, vmem_limit=VMEM_LIMIT):
    return pltpu.CompilerParams(
        dimension_semantics=("arbitrary",) * n_axes, vmem_limit_bytes=vmem_limit)


def _sigmoid(x):
    return 1.0 / (1.0 + jnp.exp(-x))


def _mod_pow2(x, n):
    assert n & (n - 1) == 0
    return x & (n - 1)


def _chunk_of(pos):
    assert CHUNK & (CHUNK - 1) == 0
    return pos >> (CHUNK.bit_length() - 1)


def _rope_lanes(sl, c_ref, s1_ref, s2_ref):
    return (sl * c_ref[...] + pltpu.roll(sl, LANES - A_ROT // 2, 1) * s1_ref[...]
            + pltpu.roll(sl, A_ROT // 2, 1) * s2_ref[...])


def _normed(x_ref, g_ref):
    x = x_ref[...]
    ms = jnp.mean(x * x, axis=-1, keepdims=True)
    return ((x * lax.rsqrt(ms + EPS)) * g_ref[...]).astype(BF16)


def _pre_rows_body(*refs, rope):
    if rope:
        x_ref, g_ref, wq_ref, wk_ref, wv_ref, wg_ref, c_ref, s1_ref, s2_ref = refs[:9]
        outs = refs[9:]
    else:
        x_ref, g_ref, wq_ref, wk_ref, wv_ref, wg_ref = refs[:6]
        outs = refs[6:]
    q_ref, k_ref, kb_ref, v_ref, vb_ref, gate_ref = outs
    hb = _normed(x_ref, g_ref)

    def rot(r, j):
        sl = r[:, j * LANES:(j + 1) * LANES]
        return _rope_lanes(sl, c_ref, s1_ref, s2_ref) if rope else sl

    q = jnp.dot(hb, wq_ref[...], preferred_element_type=F32)
    for j in range(WIDTH // LANES):
        q_ref[:, j * LANES:(j + 1) * LANES] = (rot(q, j) * Q_SCALE).astype(BF16)
    k = jnp.dot(hb, wk_ref[...], preferred_element_type=F32)
    for j in range(WIDTH // LANES):
        kr = rot(k, j)
        k_ref[:, j * LANES:(j + 1) * LANES] = kr
        kb_ref[:, j * LANES:(j + 1) * LANES] = kr.astype(BF16)
    v = jnp.dot(hb, wv_ref[...], preferred_element_type=F32)
    v_ref[...] = v
    vb_ref[...] = v.astype(BF16)
    gate_ref[...] = jnp.dot(hb, wg_ref[...], preferred_element_type=F32)


def _pre_cols_body(*refs, rope, v_cols):
    if rope:
        x_ref, g_ref, wq_ref, wkt_ref, wv_ref, wg_ref, c_ref, s1_ref, s2_ref, ct_ref, st_ref = refs[:11]
        outs = refs[11:]
    else:
        x_ref, g_ref, wq_ref, wkt_ref, wv_ref, wg_ref = refs[:6]
        outs = refs[6:]
    q_ref, kt_ref, ktb_ref, v_ref, vb_ref, gate_ref = outs
    hb = _normed(x_ref, g_ref)

    q = jnp.dot(hb, wq_ref[...], preferred_element_type=F32)
    for j in range(WIDTH // LANES):
        sl = q[:, j * LANES:(j + 1) * LANES]
        if rope:
            sl = _rope_lanes(sl, c_ref, s1_ref, s2_ref)
        q_ref[:, j * LANES:(j + 1) * LANES] = (sl * Q_SCALE).astype(BF16)

    def put_tiles(tiles_ref, val):
        for i in range(tiles_ref.shape[0]):
            tiles_ref[i] = val[:, i * KV_TILE:(i + 1) * KV_TILE].astype(BF16)

    kt = lax.dot_general(wkt_ref[...], hb, _NT, preferred_element_type=F32)
    if rope:
        half = A_ROT // 2
        assert half == SUBLANES
        cos, sin = ct_ref[...], st_ref[...]
        parts = []
        for grp in range(WIDTH // A_DH):
            r0 = grp * A_DH
            x0, x1 = kt[r0:r0 + half], kt[r0 + half:r0 + 2 * half]
            parts += [x0 * cos - x1 * sin, x1 * cos + x0 * sin, kt[r0 + 2 * half:r0 + A_DH]]
        kt = jnp.concatenate(parts, axis=0)
    kt_ref[...] = kt
    put_tiles(ktb_ref, kt)

    if v_cols:
        vt = lax.dot_general(wv_ref[...], hb, _NT, preferred_element_type=F32)
        v_ref[...] = vt
        put_tiles(vb_ref, vt)
    else:
        v = jnp.dot(hb, wv_ref[...], preferred_element_type=F32)
        v_ref[...] = v
        vb_ref[...] = v.astype(BF16)
    gate_ref[...] = jnp.dot(hb, wg_ref[...], preferred_element_type=F32)


def _split_w(w_in, t_k, t_v):
    parts = [w_in[:, i * WIDTH:(i + 1) * WIDTH] for i in range(4)]
    if t_k:
        parts[1] = parts[1].T
    if t_v:
        parts[2] = parts[2].T
    return parts


def _pre_rows(x, norm_g, w_in, tables):
    n = x.shape[0]
    rows = PRE_ROWS
    rope = tables is not None
    row_spec = lambda width: pl.BlockSpec((rows, width), lambda i: (i, 0))
    full = lambda shape: pl.BlockSpec(shape, lambda i: (0, 0))
    in_specs = [row_spec(D_MODEL), full((1, D_MODEL))] + [full((D_MODEL, WIDTH))] * 4
    args = [x, norm_g.reshape(1, D_MODEL)] + _split_w(w_in, False, False)
    if rope:
        in_specs += [full((rows, LANES))] * 3
        args += list(tables)
    out_shape = [jax.ShapeDtypeStruct((n, WIDTH), dt) for dt in (BF16, F32, BF16, F32, BF16, F32)]
    return pl.pallas_call(
        functools.partial(_pre_rows_body, rope=rope),
        grid=(n // rows,),
        in_specs=in_specs,
        out_specs=[row_spec(WIDTH)] * 6,
        out_shape=out_shape,
        compiler_params=_params(1),
        name="pre_rows_rope" if rope else "pre_rows",
    )(*args)


def _pre_cols(x, b, t, norm_g, w_in, tables, v_cols):
    rows = PRE_COLS_ROWS
    assert rows % KV_TILE == 0 and t % rows == 0
    n_t = t // rows
    rope = tables is not None
    row_spec = lambda width: pl.BlockSpec((rows, width), lambda i: (i, 0))
    full = lambda shape: pl.BlockSpec(shape, lambda i: (0, 0))
    colf_spec = pl.BlockSpec((None, WIDTH, rows), lambda i: (i // n_t, 0, i % n_t))
    colb_spec = pl.BlockSpec((None, rows // KV_TILE, WIDTH, KV_TILE), lambda i: (i // n_t, i % n_t, 0, 0))
    in_specs = [row_spec(D_MODEL), full((1, D_MODEL))] + [full((D_MODEL, WIDTH))] * 4
    args = [x, norm_g.reshape(1, D_MODEL)] + _split_w(w_in, True, v_cols)
    if rope:
        lanes_tab, rows_tab = tables
        in_specs += [pl.BlockSpec((rows, LANES), lambda i: (i % n_t, 0))] * 3
        in_specs += [pl.BlockSpec((A_ROT // 2, rows), lambda i: (0, i % n_t))] * 2
        args += list(lanes_tab) + list(rows_tab)
    colf = jax.ShapeDtypeStruct((b, WIDTH, t), F32)
    colb = jax.ShapeDtypeStruct((b, t // KV_TILE, WIDTH, KV_TILE), BF16)
    rowf = jax.ShapeDtypeStruct((b * t, WIDTH), F32)
    rowb = jax.ShapeDtypeStruct((b * t, WIDTH), BF16)
    out_shape = [rowb, colf, colb] + ([colf, colb] if v_cols else [rowf, rowb]) + [rowf]
    out_specs = ([row_spec(WIDTH), colf_spec, colb_spec]
                 + ([colf_spec, colb_spec] if v_cols else [row_spec(WIDTH)] * 2) + [row_spec(WIDTH)])
    return pl.pallas_call(
        functools.partial(_pre_cols_body, rope=rope, v_cols=v_cols),
        grid=(b * n_t,),
        in_specs=in_specs,
        out_specs=out_specs,
        out_shape=out_shape,
        compiler_params=_params(1),
        name="pre_cols_rope" if rope else "pre_cols",
    )(*args)


def _rope_tables(pos, rows):
    half = A_ROT // 2
    inv = ROPE_THETA ** (-jnp.arange(half, dtype=F32) * 2.0 / A_ROT)
    ang = pos.astype(F32)[:, None] * inv[None, :]
    cos, sin = jnp.cos(ang), jnp.sin(ang)
    t = pos.shape[0]
    c = jnp.concatenate([cos, cos, jnp.ones((t, HALF - A_ROT), F32)], axis=1)
    s1 = jnp.concatenate([-sin, jnp.zeros((t, HALF - half), F32)], axis=1)
    s2 = jnp.concatenate([jnp.zeros((t, half), F32), sin, jnp.zeros((t, HALF - A_ROT), F32)], axis=1)
    reps = (max(rows // t, 1), LANES // HALF)
    return tuple(jnp.tile(a, reps) for a in (c, s1, s2)), (cos.T, sin.T)


def _lo_lanes():
    return lax.broadcasted_iota(jnp.int32, (1, LANES), 1) < HALF


def _halves(q):
    lo = _lo_lanes()
    zero = jnp.zeros_like(q)
    return jnp.where(lo, q, zero), jnp.where(lo, zero, q)


def _lam(lam_ref, lam_init):
    a = jnp.sum(lam_ref[0:1, :] * lam_ref[1:2, :], axis=-1, keepdims=True)
    b = jnp.sum(lam_ref[2:3, :] * lam_ref[3:4, :], axis=-1, keepdims=True)
    return jnp.exp(a) - jnp.exp(b) + lam_init


def _diff_epilogue(o1, o2, lam_ref, subg_ref, g_ref, o_ref, lam_init):
    o = o1 - _lam(lam_ref, lam_init) * o2
    ms = jnp.mean(o * o, axis=-1, keepdims=True)
    o = (o * lax.rsqrt(ms + EPS)) * subg_ref[...] * (1.0 - lam_init)
    g = g_ref[...]
    o_ref[...] = (o * (g * _sigmoid(g))).astype(o_ref.dtype)


def _softplus2(z):
    sign_bit = jnp.uint32(1 << 31)
    neg_abs = lax.bitcast_convert_type(lax.bitcast_convert_type(z, jnp.uint32) | sign_bit, F32)
    return jnp.maximum(z, 0.0) + jnp.log2(1.0 + jnp.exp2(neg_abs))


def _stick_weights(z, valid, carry, tri):
    sp = _softplus2(z)
    if valid is not None:
        sp = jnp.where(valid, sp, 0.0)
    pieces = [None] * (z.shape[1] // TRI)
    for sb in range(len(pieces) - 1, -1, -1):
        cols = slice(sb * TRI, (sb + 1) * TRI)
        sps = sp[:, cols]
        suffix = jnp.dot(sps.astype(BF16), tri, preferred_element_type=F32)
        pieces[sb] = jnp.exp2(jnp.minimum(z[:, cols] - suffix, 0.0) - carry)
        carry = carry + suffix[:, :1]
    a = pieces[0] if len(pieces) == 1 else jnp.concatenate(pieces, axis=1)
    if valid is not None:
        a = jnp.where(valid, a, 0.0)
    return a, carry


def _kv_tiles(ref, kj, bk):
    per = bk // KV_TILE
    tiles = [ref[kj * per + i] for i in range(per)]
    return tiles[0] if per == 1 else jnp.concatenate(tiles, axis=1)


def _attn_a_prompt_body(lam_ref, subg_ref, q_ref, kt_ref, v_ref, g_ref, o_ref,
                        qz_ref, m_ref, l_ref, acc_ref, *, bq, bk, diag_slab, lam_init):
    qi = pl.program_id(2)
    q1, q2 = _halves(q_ref[...])
    qz_ref[0] = q1
    qz_ref[1] = q2
    m_ref[...] = jnp.full(m_ref.shape, NEG_INF, F32)
    l_ref[...] = jnp.zeros(l_ref.shape, F32)
    acc_ref[...] = jnp.zeros(acc_ref.shape, F32)

    def step(kj, diag):
        kt = _kv_tiles(kt_ref, kj, bk)
        v = v_ref[pl.ds(pl.multiple_of(kj * bk, bk), bk), :]
        if diag is not None:
            v = jnp.concatenate([v, jnp.ones_like(v)], axis=1)
        slab = bq if diag is None else diag_slab
        for r0 in range(0, bq, slab):
            if diag is None or _chunk_of((diag + 1) * bk - 1) <= _chunk_of(r0):
                vis = None
            elif _chunk_of(diag * bk) > _chunk_of(r0 + slab - 1):
                continue
            else:
                qpos = r0 + lax.broadcasted_iota(jnp.int32, (slab, bk), 0)
                kpos = diag * bk + lax.broadcasted_iota(jnp.int32, (slab, bk), 1)
                vis = _chunk_of(kpos) <= _chunk_of(qpos)
            rows = slice(r0, r0 + slab)
            for c in range(2):
                s = jnp.dot(qz_ref[c, rows, :], kt, preferred_element_type=F32)
                if vis is not None:
                    s = jnp.where(vis, s, NEG_INF)
                m_prev = m_ref[c, rows, :]
                m_new = jnp.maximum(m_prev, jnp.max(s, axis=-1, keepdims=True))
                alpha = jnp.exp2(m_prev - m_new)
                p = jnp.exp2(s - jnp.tile(m_new, (1, bk // LANES)))
                pv = jnp.dot(p.astype(BF16), v, preferred_element_type=F32)
                psum = jnp.sum(p, axis=-1, keepdims=True) if diag is None else pv[:, LANES:]
                l_ref[c, rows, :] = alpha * l_ref[c, rows, :] + psum
                acc_ref[c, rows, :] = alpha * acc_ref[c, rows, :] + pv[:, :LANES]
                m_ref[c, rows, :] = m_new

    n_full = qi * (bq // bk)

    def body(kj, c):
        step(kj, None)
        return c

    lax.fori_loop(0, n_full, body, 0)
    for r in range(bq // bk):
        step(n_full + r, r)

    _diff_epilogue(acc_ref[0] / l_ref[0], acc_ref[1] / l_ref[1],
                   lam_ref, subg_ref, g_ref, o_ref, lam_init)


def _attn_a_prompt(q, ktb, vb, gate, lam4, subg, lam_init):
    b, t, _ = q.shape
    bq, bk = A_BQ, A_BK
    assert bq % bk == 0 and bk % KV_TILE == 0 and t % bq == 0 and bk % CHUNK == 0
    q_spec = pl.BlockSpec((None, bq, LANES), lambda bi, h, qi: (bi, qi, h))
    kt_spec = pl.BlockSpec((None, t // KV_TILE, LANES, KV_TILE), lambda bi, h, qi: (bi, 0, h, 0))
    v_spec = pl.BlockSpec((None, t, LANES), lambda bi, h, qi: (bi, 0, h))
    full = lambda shape: pl.BlockSpec(shape, lambda bi, h, qi: (0,) * len(shape))
    return pl.pallas_call(
        functools.partial(_attn_a_prompt_body, bq=bq, bk=bk, diag_slab=A_SLAB, lam_init=lam_init),
        grid=(b, A_HEADS, t // bq),
        in_specs=[full(lam4.shape), full(subg.shape), q_spec, kt_spec, v_spec, q_spec],
        out_specs=q_spec,
        out_shape=jax.ShapeDtypeStruct((b, t, WIDTH), BF16),
        scratch_shapes=[pltpu.VMEM((2, bq, LANES), BF16),
                        pltpu.VMEM((2, bq, LANES), F32),
                        pltpu.VMEM((2, bq, LANES), F32),
                        pltpu.VMEM((2, bq, LANES), F32)],
        compiler_params=_params(3),
        name="attn_a_prompt",
    )(lam4, subg, q, ktb, vb, gate)


def _attn_b_prompt_body(tri_ref, q_ref, kt_ref, vt_ref, g_ref, o_ref,
                        qz_ref, carry_ref, acc_ref, *, bq, sub):
    qi = pl.program_id(2)
    q1, q2 = _halves(q_ref[...])
    qz_ref[0] = q1
    qz_ref[1] = q2
    carry_ref[...] = jnp.zeros(carry_ref.shape, F32)
    acc_ref[...] = jnp.zeros(acc_ref.shape, F32)

    def fold(rows, kt, vt, valid):
        n_sb = kt.shape[1] // TRI
        zs = [jnp.dot(qz_ref[c, rows, :], kt, preferred_element_type=F32) for c in range(2)]
        sps = [_softplus2(z) for z in zs]
        if valid is not None:
            sps = [jnp.where(valid, sp, 0.0) for sp in sps]
        carries = [carry_ref[c, rows, :] for c in range(2)]
        pieces = [[None] * n_sb for _ in range(2)]
        for sb in range(n_sb - 1, -1, -1):
            cols = slice(sb * TRI, (sb + 1) * TRI)
            sfx = [jnp.dot(sp[:, cols].astype(BF16), tri_ref[...], preferred_element_type=F32) for sp in sps]
            for c in range(2):
                pieces[c][sb] = jnp.exp2(jnp.minimum(zs[c][:, cols] - sfx[c], 0.0) - carries[c])
                carries[c] = carries[c] + sfx[c][:, :1]
        for c in range(2):
            a = pieces[c][0] if n_sb == 1 else jnp.concatenate(pieces[c], axis=1)
            if valid is not None:
                a = jnp.where(valid, a, 0.0)
            acc_ref[c, rows, :] += lax.dot_general(a.astype(BF16), vt, _NT, preferred_element_type=F32)
            carry_ref[c, rows, :] = carries[c]

    assert sub == KV_TILE
    row = lax.broadcasted_iota(jnp.int32, (sub, 2 * sub), 0)
    col = lax.broadcasted_iota(jnp.int32, (sub, 2 * sub), 1)
    causal = col < row + sub
    for j in range(bq // sub):
        rows = slice(j * sub, (j + 1) * sub)
        own = qi * (bq // sub) + j
        prev = jnp.maximum(own - 1, 0)
        kt = jnp.concatenate([kt_ref[prev], kt_ref[own]], axis=1)
        vt = jnp.concatenate([vt_ref[prev], vt_ref[own]], axis=1)
        valid = jnp.logical_and(causal, jnp.logical_or(col >= sub, own > 0))
        fold(rows, kt, vt, valid)

    for j in range(bq // sub):
        rows = slice(j * sub, (j + 1) * sub)
        own = qi * (bq // sub) + j

        def exhausted(rows=rows):
            return jnp.min(carry_ref[:, rows, :]) >= STICK_EXIT

        def cond(state):
            t, done = state
            return jnp.logical_and(t >= 0, jnp.logical_not(done))

        def body(state, rows=rows, exhausted=exhausted):
            t, _ = state
            fold(rows, kt_ref[t], vt_ref[t], None)
            return t - 1, exhausted()

        lax.while_loop(cond, body, (own - 2, exhausted()))

    g = g_ref[...]
    o = jnp.where(_lo_lanes(), acc_ref[0], acc_ref[1])
    o_ref[...] = (o * (g * _sigmoid(g))).astype(o_ref.dtype)


def _attn_b_prompt(q, ktb, vtb, gate, tri):
    b, t, _ = q.shape
    bq = B_BQ
    assert bq % KV_TILE == 0 and t % bq == 0 and KV_TILE % TRI == 0
    q_spec = pl.BlockSpec((None, bq, LANES), lambda bi, h, qi: (bi, qi, h))
    kt_spec = pl.BlockSpec((None, t // KV_TILE, LANES, KV_TILE), lambda bi, h, qi: (bi, 0, h, 0))
    return pl.pallas_call(
        functools.partial(_attn_b_prompt_body, bq=bq, sub=KV_TILE),
        grid=(b, WIDTH // LANES, t // bq),
        in_specs=[pl.BlockSpec((TRI, TRI), lambda bi, h, qi: (0, 0)),
                  q_spec, kt_spec, kt_spec, q_spec],
        out_specs=q_spec,
        out_shape=jax.ShapeDtypeStruct((b, t, WIDTH), BF16),
        scratch_shapes=[pltpu.VMEM((2, bq, LANES), BF16),
                        pltpu.VMEM((2, bq, 1), F32),
                        pltpu.VMEM((2, bq, LANES), F32)],
        compiler_params=_params(3),
        name="attn_b_prompt",
    )(tri, q, ktb, vtb, gate)


def _pad_rows(x, rows):
    return jnp.concatenate([x, jnp.zeros((rows - x.shape[0], x.shape[1]), x.dtype)], axis=0)


def _attn_a_sample_body(lam_ref, subg_ref, q_ref, kn_ref, vn_ref, ktp_ref, vp_ref, g_ref, o_ref,
                        qq_ref, m_ref, l_ref, acc_ref, *, t_new, past, chunk, lam_init):
    pc = pl.program_id(1)

    @pl.when(pc == 0)
    def _():
        for h in range(A_HEADS):
            qq_ref[h] = jnp.concatenate(_halves(q_ref[:, h * LANES:(h + 1) * LANES]), axis=0)
        m_ref[...] = jnp.full(m_ref.shape, NEG_INF, F32)
        l_ref[...] = jnp.zeros(l_ref.shape, F32)
        acc_ref[...] = jnp.zeros(acc_ref.shape, F32)

    def update(h, s, v):
        m_prev = m_ref[h]
        m_new = jnp.maximum(m_prev, jnp.max(s, axis=-1, keepdims=True))
        alpha = jnp.exp2(m_prev - m_new)
        p = jnp.exp2(s - jnp.tile(m_new, (1, s.shape[1] // LANES)))
        l_ref[h] = alpha * l_ref[h] + jnp.sum(p, axis=-1, keepdims=True)
        acc_ref[h] = alpha * acc_ref[h] + jnp.dot(p.astype(BF16), v, preferred_element_type=F32)
        m_ref[h] = m_new

    for h in range(A_HEADS):
        kt = ktp_ref[h * LANES:(h + 1) * LANES, :].astype(BF16)
        v = vp_ref[pl.ds(h, chunk, stride=A_HEADS), :].astype(BF16)
        update(h, jnp.dot(qq_ref[h], kt, preferred_element_type=F32), v)

    @pl.when(pc == pl.num_programs(1) - 1)
    def _():
        row = lax.broadcasted_iota(jnp.int32, (2 * t_new, LANES), 0)
        col = lax.broadcasted_iota(jnp.int32, (2 * t_new, LANES), 1)
        vis = jnp.logical_and(col < t_new,
                              _chunk_of(past + col) <= _chunk_of(past + _mod_pow2(row, t_new)))
        for h in range(A_HEADS):
            cols = slice(h * LANES, (h + 1) * LANES)
            kn = _pad_rows(kn_ref[:, cols], LANES)
            vn = _pad_rows(vn_ref[:, cols], LANES)
            s_n = lax.dot_general(qq_ref[h], kn, _NT, preferred_element_type=F32)
            update(h, jnp.where(vis, s_n, NEG_INF), vn)
            o = acc_ref[h] / l_ref[h]
            _diff_epilogue(o[:t_new], o[t_new:], lam_ref, subg_ref, g_ref.at[:, cols], o_ref.at[:, cols],
                           lam_init)


def _attn_a_sample(q, kb, vb, kt_past, v_past, gate, lam4, subg, lam_init):
    b, t_new, _ = q.shape
    past = kt_past.shape[2]
    chunk = SAMPLE_CHUNK
    assert v_past.shape[1] == past * A_HEADS and past % chunk == 0
    new_spec = pl.BlockSpec((None, t_new, WIDTH), lambda bi, pc: (bi, 0, 0))
    ktp_spec = pl.BlockSpec((None, WIDTH, chunk), lambda bi, pc: (bi, 0, pc))
    vp_spec = pl.BlockSpec((None, chunk * A_HEADS, LANES), lambda bi, pc: (bi, pc, 0))
    full = lambda shape: pl.BlockSpec(shape, lambda bi, pc: (0,) * len(shape))
    state = pltpu.VMEM((A_HEADS, 2 * t_new, LANES), F32)
    return pl.pallas_call(
        functools.partial(_attn_a_sample_body, t_new=t_new, past=past, chunk=chunk, lam_init=lam_init),
        grid=(b, past // chunk),
        in_specs=[full(lam4.shape), full(subg.shape), new_spec, new_spec, new_spec,
                  ktp_spec, vp_spec, new_spec],
        out_specs=new_spec,
        out_shape=jax.ShapeDtypeStruct((b, t_new, WIDTH), BF16),
        scratch_shapes=[pltpu.VMEM((A_HEADS, 2 * t_new, LANES), BF16), state, state, state],
        compiler_params=_params(2),
        name="attn_a_sample",
    )(lam4, subg, q, kb, vb, kt_past, v_past, gate)


def _attn_b_sample_body(tri_ref, q_ref, kn_ref, vn_ref, g_ref, kt_hbm, vt_hbm, o_ref,
                        kbuf, vbuf, sems, qq_ref, carry_ref, acc_ref, *, t_new, past, chunk):
    bi = pl.program_id(0)
    rows = 2 * t_new
    n_blk = chunk // TRI
    n_chunks = past // chunk
    n_pairs = WIDTH // LANES
    slot = lax.rem(bi, 2)
    tri = tri_ref[...]

    def fetch(batch, ci, into):
        cols = pl.ds(pl.multiple_of(ci * chunk, chunk), chunk)
        return (pltpu.make_async_copy(kt_hbm.at[batch, :, cols], kbuf.at[into], sems.at[0, into]),
                pltpu.make_async_copy(vt_hbm.at[batch, :, cols], vbuf.at[into], sems.at[1, into]))

    @pl.when(bi == 0)
    def _():
        for cp in fetch(0, n_chunks - 1, 0):
            cp.start()

    @pl.when(bi + 1 < pl.num_programs(0))
    def _():
        for cp in fetch(bi + 1, n_chunks - 1, 1 - slot):
            cp.start()

    row = lax.broadcasted_iota(jnp.int32, (rows, TRI), 0)
    col = lax.broadcasted_iota(jnp.int32, (rows, TRI), 1)
    valid = col < _mod_pow2(row, t_new)
    for hp in range(n_pairs):
        cols = slice(hp * LANES, (hp + 1) * LANES)
        qq = jnp.concatenate(_halves(q_ref[:, cols]), axis=0)
        qq_ref[hp] = qq
        kn = _pad_rows(kn_ref[:, cols], TRI)
        vn = _pad_rows(vn_ref[:, cols], TRI)
        z_n = lax.dot_general(qq, kn, _NT, preferred_element_type=F32)
        a_n, carry = _stick_weights(z_n, valid, jnp.zeros((rows, 1), F32), tri)
        acc_ref[hp] = jnp.dot(a_n.astype(BF16), vn, preferred_element_type=F32)
        carry_ref[hp] = carry

    def absorb():
        for hp in range(n_pairs):
            feat = pl.ds(hp * LANES, LANES)
            z_all = jnp.dot(qq_ref[hp], kbuf[slot, feat, :].astype(BF16), preferred_element_type=F32)
            z = jnp.concatenate([z_all[:, c * TRI:(c + 1) * TRI] for c in range(n_blk)], axis=0)
            suffix = jnp.dot(_softplus2(z).astype(BF16), tri, preferred_element_type=F32)
            running = carry_ref[hp]
            carries = [None] * n_blk
            for c in range(n_blk - 1, -1, -1):
                carries[c] = running
                running = running + suffix[c * rows:(c + 1) * rows, :1]
            carry_ref[hp] = running
            a = jnp.exp2(jnp.minimum(z - suffix, 0.0) - jnp.concatenate(carries, axis=0)).astype(BF16)
            vtp = vbuf[slot, feat, :].astype(BF16)
            acc = acc_ref[hp]
            for c in range(n_blk):
                acc = acc + lax.dot_general(a[c * rows:(c + 1) * rows], vtp[:, c * TRI:(c + 1) * TRI], _NT,
                                            preferred_element_type=F32)
            acc_ref[hp] = acc

    def exhausted():
        return jnp.min(carry_ref[...]) >= STICK_EXIT

    for cp in fetch(bi, n_chunks - 1, slot):
        cp.wait()
    absorb()

    def cond(state):
        ci, done = state
        return jnp.logical_and(ci >= 0, jnp.logical_not(done))

    def body(state):
        ci, _ = state
        copies = fetch(bi, ci, slot)
        for cp in copies:
            cp.start()
        for cp in copies:
            cp.wait()
        absorb()
        return ci - 1, exhausted()

    lax.while_loop(cond, body, (jnp.int32(n_chunks - 2), exhausted()))

    for hp in range(n_pairs):
        cols = slice(hp * LANES, (hp + 1) * LANES)
        acc = acc_ref[hp]
        g = g_ref[:, cols]
        o = jnp.where(_lo_lanes(), acc[:t_new], acc[t_new:])
        o_ref[:, cols] = (o * (g * _sigmoid(g))).astype(o_ref.dtype)


def _attn_b_sample(q, kb, vb, kt_past, vt_past, gate, tri):
    b, t_new, _ = q.shape
    past = kt_past.shape[2]
    chunk = SAMPLE_B_CHUNK
    assert past % chunk == 0 and chunk % TRI == 0 and t_new <= TRI
    n_pairs = WIDTH // LANES
    new_spec = pl.BlockSpec((None, t_new, WIDTH), lambda bi: (bi, 0, 0))
    hbm_spec = pl.BlockSpec(memory_space=pl.ANY)
    return pl.pallas_call(
        functools.partial(_attn_b_sample_body, t_new=t_new, past=past, chunk=chunk),
        grid=(b,),
        in_specs=[pl.BlockSpec((TRI, TRI), lambda bi: (0, 0)),
                  new_spec, new_spec, new_spec, new_spec, hbm_spec, hbm_spec],
        out_specs=new_spec,
        out_shape=jax.ShapeDtypeStruct((b, t_new, WIDTH), BF16),
        scratch_shapes=[pltpu.VMEM((2, WIDTH, chunk), F32),
                        pltpu.VMEM((2, WIDTH, chunk), F32),
                        pltpu.SemaphoreType.DMA((2, 2)),
                        pltpu.VMEM((n_pairs, 2 * t_new, LANES), BF16),
                        pltpu.VMEM((n_pairs, 2 * t_new, 1), F32),
                        pltpu.VMEM((n_pairs, 2 * t_new, LANES), F32)],
        compiler_params=_params(1),
        name="attn_b_sample",
    )(tri, q, kb, vb, gate, kt_past, vt_past)


def _post_body(*refs, final):
    if final:
        o_ref, x_ref, p_ref, wo_ref, wg_ref, wp_ref, fg_ref, out_ref = refs
    else:
        o_ref, x_ref, p_ref, wo_ref, wg_ref, wp_ref, out_ref = refs
    x = x_ref[...] + jnp.dot(o_ref[...], wo_ref[...], preferred_element_type=F32)
    gate = _sigmoid(jnp.dot(x.astype(BF16), wg_ref[...], preferred_element_type=F32))
    x = x + gate * jnp.dot(p_ref[...].astype(BF16), wp_ref[...], preferred_element_type=F32)
    if final:
        ms = jnp.mean(x * x, axis=-1, keepdims=True)
        x = (x * lax.rsqrt(ms + EPS)) * fg_ref[...]
    out_ref[...] = x


def _post(o, x, p, layer, w_out, w_gate, w_proj, final_g):
    n = x.shape[0]
    rows = POST_ROWS
    assert n % rows == 0
    final = final_g is not None
    row_spec = lambda width: pl.BlockSpec((rows, width), lambda i: (i, 0))
    full = lambda shape: pl.BlockSpec(shape, lambda i: (0, 0))
    in_specs = [row_spec(WIDTH), row_spec(D_MODEL),
                pl.BlockSpec((None, rows, PLE_DIM), lambda i: (layer, i, 0)),
                full(w_out.shape), full(w_gate.shape), full(w_proj.shape)]
    args = [o, x, p, w_out, w_gate, w_proj]
    if final:
        in_specs.append(full((1, D_MODEL)))
        args.append(final_g.reshape(1, D_MODEL))
    return pl.pallas_call(
        functools.partial(_post_body, final=final),
        grid=(n // rows,),
        in_specs=in_specs,
        out_specs=row_spec(D_MODEL),
        out_shape=jax.ShapeDtypeStruct((n, D_MODEL), F32),
        compiler_params=_params(1),
        name="post_final" if final else "post",
    )(*args)


def _mixer_order(depth):
    return [("a", i // 2) if i % 2 == 0 else ("b", i // 2) for i in range(depth)]


def _lam_init(i):
    return 0.8 - 0.6 * math.exp(-0.3 * i)


def _trunk_prompt(x, p, pos, weights):
    (a_norm_g, a_w_in, a_lam, a_subln_g, a_w_out, b_norm_g, b_w_in, b_w_out,
     ple_w_proj, ple_w_gate, final_norm_g) = weights
    b, t, _ = x.shape
    n = b * t
    depth = p.shape[0]
    xf = x.reshape(n, D_MODEL)
    tables = _rope_tables(pos, PRE_ROWS)
    tri = jnp.tri(TRI, dtype=BF16)
    shp = (b, t, WIDTH)
    ak, av, bk, bv = [], [], [], []
    for i, (kind, j) in enumerate(_mixer_order(depth)):
        if kind == "a":
            q, kt, ktb, v, vb, gate = _pre_cols(xf, b, t, a_norm_g[j], a_w_in[j], tables, False)
            o = _attn_a_prompt(q.reshape(shp), ktb, vb.reshape(shp), gate.reshape(shp),
                               a_lam[j], a_subln_g[j].reshape(1, 2 * A_DH), _lam_init(i))
            ak.append(kt.reshape(b, A_HEADS, 2, A_DH, t).transpose(0, 4, 1, 2, 3))
            av.append(v.reshape(b, t, A_HEADS, 2 * A_DH))
            w_out = a_w_out[j]
        else:
            q, kt, ktb, vt, vtb, gate = _pre_cols(xf, b, t, b_norm_g[j], b_w_in[j], None, True)
            o = _attn_b_prompt(q.reshape(shp), ktb, vtb, gate.reshape(shp), tri)
            bk.append(kt.reshape(b, B_HEADS, B_DH, t).transpose(0, 3, 1, 2))
            bv.append(vt.reshape(b, B_HEADS, B_DH, t).transpose(0, 3, 1, 2))
            w_out = b_w_out[j]
        xf = _post(o.reshape(n, WIDTH), xf, p.reshape(depth, n, PLE_DIM), i, w_out,
                   ple_w_gate[i], ple_w_proj[i], final_norm_g if i == depth - 1 else None)
    return (xf.reshape(b, t, D_MODEL), jnp.stack(ak), jnp.stack(av), jnp.stack(bk), jnp.stack(bv))


def _trunk_sample(x, p, pos, weights, caches):
    (a_norm_g, a_w_in, a_lam, a_subln_g, a_w_out, b_norm_g, b_w_in, b_w_out,
     ple_w_proj, ple_w_gate, final_norm_g) = weights
    cache_a_k, cache_a_v, cache_b_k, cache_b_v = caches
    b, t, _ = x.shape
    n = b * t
    depth = p.shape[0]
    past = cache_a_k.shape[2]
    xf = x.reshape(n, D_MODEL)
    lane_tables, _ = _rope_tables(pos, PRE_ROWS)
    tri = jnp.tri(TRI, dtype=BF16)
    shp = (b, t, WIDTH)
    ak, av, bk, bv = [], [], [], []
    for i, (kind, j) in enumerate(_mixer_order(depth)):
        if kind == "a":
            q, k, kb, v, vb, gate = _pre_rows(xf, a_norm_g[j], a_w_in[j], lane_tables)
            kt_past = cache_a_k[j].transpose(0, 2, 3, 4, 1).reshape(b, WIDTH, past)
            v_past = cache_a_v[j].reshape(b, past * A_HEADS, 2 * A_DH)
            o = _attn_a_sample(q.reshape(shp), kb.reshape(shp), vb.reshape(shp), kt_past, v_past,
                               gate.reshape(shp), a_lam[j], a_subln_g[j].reshape(1, 2 * A_DH), _lam_init(i))
            ak.append(k.reshape(b, t, A_HEADS, 2, A_DH))
            av.append(v.reshape(b, t, A_HEADS, 2 * A_DH))
            w_out = a_w_out[j]
        else:
            q, k, kb, v, vb, gate = _pre_rows(xf, b_norm_g[j], b_w_in[j], None)
            kt_past = cache_b_k[j].transpose(0, 2, 3, 1).reshape(b, WIDTH, past)
            vt_past = cache_b_v[j].transpose(0, 2, 3, 1).reshape(b, WIDTH, past)
            o = _attn_b_sample(q.reshape(shp), kb.reshape(shp), vb.reshape(shp), kt_past, vt_past,
                               gate.reshape(shp), tri)
            bk.append(k.reshape(b, t, B_HEADS, B_DH))
            bv.append(v.reshape(b, t, B_HEADS, B_DH))
            w_out = b_w_out[j]
        xf = _post(o.reshape(n, WIDTH), xf, p.reshape(depth, n, PLE_DIM), i, w_out,
                   ple_w_gate[i], ple_w_proj[i], final_norm_g if i == depth - 1 else None)
    return (xf.reshape(b, t, D_MODEL), jnp.stack(ak), jnp.stack(av), jnp.stack(bk), jnp.stack(bv))


def kernel(x_prompt, x_sample, cache_a_k, cache_a_v, cache_b_k, cache_b_v, p_prompt, p_sample,
           a_norm_g, a_w_in, a_lam_q1, a_lam_k1, a_lam_q2, a_lam_k2, a_subln_g, a_w_out,
           b_norm_g, b_w_in, b_w_out, ple_w_proj, ple_w_gate, final_norm_g):
    past = cache_a_k.shape[2]
    a_lam = jnp.stack([a_lam_q1, a_lam_k1, a_lam_q2, a_lam_k2], axis=1)
    weights = (a_norm_g, a_w_in.astype(BF16), a_lam, a_subln_g, a_w_out.astype(BF16),
               b_norm_g, b_w_in.astype(BF16), b_w_out.astype(BF16),
               ple_w_proj.astype(BF16), ple_w_gate.astype(BF16), final_norm_g)
    pos_p = jnp.arange(x_prompt.shape[1], dtype=jnp.int32)
    pos_s = past + jnp.arange(x_sample.shape[1], dtype=jnp.int32)
    y_s, ak_s, av_s, bk_s, bv_s = _trunk_sample(x_sample, p_sample, pos_s, weights,
                                                (cache_a_k, cache_a_v, cache_b_k, cache_b_v))
    y_p, ak_p, av_p, bk_p, bv_p = _trunk_prompt(x_prompt, p_prompt, pos_p, weights)
    return (y_p, y_s, ak_p, av_p, bk_p, bv_p, ak_s, av_s, bk_s, bv_s)
```

```python
import functools
import math

import jax
import jax.numpy as jnp
from jax import lax
from jax.experimental import pallas as pl
from jax.experimental.pallas import tpu as pltpu

F32 = jnp.float32
BF16 = jnp.bfloat16

D_MODEL = 1024
CHUNK = 64
PLE_DIM = 256
ROPE_THETA = 500000.0
EPS = 1e-6
NEG_INF = -1e30
A_HEADS = 8
A_DH = 64
A_ROT = A_DH // 4
B_HEADS = 16
B_DH = 64
WIDTH = 1024
LANES = 128
SUBLANES = 8
HALF = 64
Q_SCALE = A_DH ** -0.5 * math.log2(math.e)

PRE_ROWS = 256
PRE_COLS_ROWS = 512
POST_ROWS = 512
KV_TILE = 256
TRI = 256
A_BQ, A_BK = 2048, 512
B_BQ = 2048
A_SLAB = 512
STICK_EXIT = 151.0
SAMPLE_B_CHUNK = 256
SAMPLE_CHUNK = 2048
VMEM_LIMIT = 48 * 1024 * 1024

_NT = (((1,), (1,)), ((), ()))


def _params(n_axes, vmem_limit=VMEM_LIMIT):
    return pltpu.CompilerParams(
        dimension_semantics=("arbitrary",) * n_axes, vmem_limit_bytes=vmem_limit)


def _sigmoid(x):
    return 1.0 / (1.0 + jnp.exp(-x))


def _mod_pow2(x, n):
    assert n & (n - 1) == 0
    return x & (n - 1)


def _chunk_of(pos):
    assert CHUNK & (CHUNK - 1) == 0
    return pos >> (CHUNK.bit_length() - 1)


def _rope_lanes(sl, c_ref, s1_ref, s2_ref):
    return (sl * c_ref[...] + pltpu.roll(sl, LANES - A_ROT // 2, 1) * s1_ref[...]
            + pltpu.roll(sl, A_ROT // 2, 1) * s2_ref[...])


def _normed(x_ref, g_ref):
    x = x_ref[...]
    ms = jnp.mean(x * x, axis=-1, keepdims=True)
    return ((x * lax.rsqrt(ms + EPS)) * g_ref[...]).astype(BF16)


def _pre_rows_body(*refs, rope):
    if rope:
        x_ref, g_ref, wq_ref, wk_ref, wv_ref, wg_ref, c_ref, s1_ref, s2_ref = refs[:9]
        outs = refs[9:]
    else:
        x_ref, g_ref, wq_ref, wk_ref, wv_ref, wg_ref = refs[:6]
        outs = refs[6:]
    q_ref, k_ref, kb_ref, v_ref, vb_ref, gate_ref = outs
    hb = _normed(x_ref, g_ref)

    def rot(r, j):
        sl = r[:, j * LANES:(j + 1) * LANES]
        return _rope_lanes(sl, c_ref, s1_ref, s2_ref) if rope else sl

    q = jnp.dot(hb, wq_ref[...], preferred_element_type=F32)
    for j in range(WIDTH // LANES):
        q_ref[:, j * LANES:(j + 1) * LANES] = (rot(q, j) * Q_SCALE).astype(BF16)
    k = jnp.dot(hb, wk_ref[...], preferred_element_type=F32)
    for j in range(WIDTH // LANES):
        kr = rot(k, j)
        k_ref[:, j * LANES:(j + 1) * LANES] = kr
        kb_ref[:, j * LANES:(j + 1) * LANES] = kr.astype(BF16)
    v = jnp.dot(hb, wv_ref[...], preferred_element_type=F32)
    v_ref[...] = v
    vb_ref[...] = v.astype(BF16)
    gate_ref[...] = jnp.dot(hb, wg_ref[...], preferred_element_type=F32)


def _pre_cols_body(*refs, rope, v_cols):
    if rope:
        x_ref, g_ref, wq_ref, wkt_ref, wv_ref, wg_ref, c_ref, s1_ref, s2_ref, ct_ref, st_ref = refs[:11]
        outs = refs[11:]
    else:
        x_ref, g_ref, wq_ref, wkt_ref, wv_ref, wg_ref = refs[:6]
        outs = refs[6:]
    q_ref, kt_ref, ktb_ref, v_ref, vb_ref, gate_ref = outs
    hb = _normed(x_ref, g_ref)

    q = jnp.dot(hb, wq_ref[...], preferred_element_type=F32)
    for j in range(WIDTH // LANES):
        sl = q[:, j * LANES:(j + 1) * LANES]
        if rope:
            sl = _rope_lanes(sl, c_ref, s1_ref, s2_ref)
        q_ref[:, j * LANES:(j + 1) * LANES] = (sl * Q_SCALE).astype(BF16)

    def put_tiles(tiles_ref, val):
        for i in range(tiles_ref.shape[0]):
            tiles_ref[i] = val[:, i * KV_TILE:(i + 1) * KV_TILE].astype(BF16)

    kt = lax.dot_general(wkt_ref[...], hb, _NT, preferred_element_type=F32)
    if rope:
        half = A_ROT // 2
        assert half == SUBLANES
        cos, sin = ct_ref[...], st_ref[...]
        parts = []
        for grp in range(WIDTH // A_DH):
            r0 = grp * A_DH
            x0, x1 = kt[r0:r0 + half], kt[r0 + half:r0 + 2 * half]
            parts += [x0 * cos - x1 * sin, x1 * cos + x0 * sin, kt[r0 + 2 * half:r0 + A_DH]]
        kt = jnp.concatenate(parts, axis=0)
    kt_ref[...] = kt
    put_tiles(ktb_ref, kt)

    if v_cols:
        vt = lax.dot_general(wv_ref[...], hb, _NT, preferred_element_type=F32)
        v_ref[...] = vt
        put_tiles(vb_ref, vt)
    else:
        v = jnp.dot(hb, wv_ref[...], preferred_element_type=F32)
        v_ref[...] = v
        vb_ref[...] = v.astype(BF16)
    gate_ref[...] = jnp.dot(hb, wg_ref[...], preferred_element_type=F32)


def _split_w(w_in, t_k, t_v):
    parts = [w_in[:, i * WIDTH:(i + 1) * WIDTH] for i in range(4)]
    if t_k:
        parts[1] = parts[1].T
    if t_v:
        parts[2] = parts[2].T
    return parts


def _pre_rows(x, norm_g, w_in, tables):
    n = x.shape[0]
    rows = PRE_ROWS
    rope = tables is not None
    row_spec = lambda width: pl.BlockSpec((rows, width), lambda i: (i, 0))
    full = lambda shape: pl.BlockSpec(shape, lambda i: (0, 0))
    in_specs = [row_spec(D_MODEL), full((1, D_MODEL))] + [full((D_MODEL, WIDTH))] * 4
    args = [x, norm_g.reshape(1, D_MODEL)] + _split_w(w_in, False, False)
    if rope:
        in_specs += [full((rows, LANES))] * 3
        args += list(tables)
    out_shape = [jax.ShapeDtypeStruct((n, WIDTH), dt) for dt in (BF16, F32, BF16, F32, BF16, F32)]
    return pl.pallas_call(
        functools.partial(_pre_rows_body, rope=rope),
        grid=(n // rows,),
        in_specs=in_specs,
        out_specs=[row_spec(WIDTH)] * 6,
        out_shape=out_shape,
        compiler_params=_params(1),
        name="pre_rows_rope" if rope else "pre_rows",
    )(*args)


def _pre_cols(x, b, t, norm_g, w_in, tables, v_cols):
    rows = PRE_COLS_ROWS
    assert rows % KV_TILE == 0 and t % rows == 0
    n_t = t // rows
    rope = tables is not None
    row_spec = lambda width: pl.BlockSpec((rows, width), lambda i: (i, 0))
    full = lambda shape: pl.BlockSpec(shape, lambda i: (0, 0))
    colf_spec = pl.BlockSpec((None, WIDTH, rows), lambda i: (i // n_t, 0, i % n_t))
    colb_spec = pl.BlockSpec((None, rows // KV_TILE, WIDTH, KV_TILE), lambda i: (i // n_t, i % n_t, 0, 0))
    in_specs = [row_spec(D_MODEL), full((1, D_MODEL))] + [full((D_MODEL, WIDTH))] * 4
    args = [x, norm_g.reshape(1, D_MODEL)] + _split_w(w_in, True, v_cols)
    if rope:
        lanes_tab, rows_tab = tables
        in_specs += [pl.BlockSpec((rows, LANES), lambda i: (i % n_t, 0))] * 3
        in_specs += [pl.BlockSpec((A_ROT // 2, rows), lambda i: (0, i % n_t))] * 2
        args += list(lanes_tab) + list(rows_tab)
    colf = jax.ShapeDtypeStruct((b, WIDTH, t), F32)
    colb = jax.ShapeDtypeStruct((b, t // KV_TILE, WIDTH, KV_TILE), BF16)
    rowf = jax.ShapeDtypeStruct((b * t, WIDTH), F32)
    rowb = jax.ShapeDtypeStruct((b * t, WIDTH), BF16)
    out_shape = [rowb, colf, colb] + ([colf, colb] if v_cols else [rowf, rowb]) + [rowf]
    out_specs = ([row_spec(WIDTH), colf_spec, colb_spec]
                 + ([colf_spec, colb_spec] if v_cols else [row_spec(WIDTH)] * 2) + [row_spec(WIDTH)])
    return pl.pallas_call(
        functools.partial(_pre_cols_body, rope=rope, v_cols=v_cols),
        grid=(b * n_t,),
        in_specs=in_specs,
        out_specs=out_specs,
        out_shape=out_shape,
        compiler_params=_params(1),
        name="pre_cols_rope" if rope else "pre_cols",
    )(*args)


def _rope_tables(pos, rows):
    half = A_ROT // 2
    inv = ROPE_THETA ** (-jnp.arange(half, dtype=F32) * 2.0 / A_ROT)
    ang = pos.astype(F32)[:, None] * inv[None, :]
    cos, sin = jnp.cos(ang), jnp.sin(ang)
    t = pos.shape[0]
    c = jnp.concatenate([cos, cos, jnp.ones((t, HALF - A_ROT), F32)], axis=1)
    s1 = jnp.concatenate([-sin, jnp.zeros((t, HALF - half), F32)], axis=1)
    s2 = jnp.concatenate([jnp.zeros((t, half), F32), sin, jnp.zeros((t, HALF - A_ROT), F32)], axis=1)
    reps = (max(rows // t, 1), LANES // HALF)
    return tuple(jnp.tile(a, reps) for a in (c, s1, s2)), (cos.T, sin.T)


def _lo_lanes():
    return lax.broadcasted_iota(jnp.int32, (1, LANES), 1) < HALF


def _halves(q):
    lo = _lo_lanes()
    zero = jnp.zeros_like(q)
    return jnp.where(lo, q, zero), jnp.where(lo, zero, q)


def _lam(lam_ref, lam_init):
    a = jnp.sum(lam_ref[0:1, :] * lam_ref[1:2, :], axis=-1, keepdims=True)
    b = jnp.sum(lam_ref[2:3, :] * lam_ref[3:4, :], axis=-1, keepdims=True)
    return jnp.exp(a) - jnp.exp(b) + lam_init


def _diff_epilogue(o1, o2, lam_ref, subg_ref, g_ref, o_ref, lam_init):
    o = o1 - _lam(lam_ref, lam_init) * o2
    ms = jnp.mean(o * o, axis=-1, keepdims=True)
    o = (o * lax.rsqrt(ms + EPS)) * subg_ref[...] * (1.0 - lam_init)
    g = g_ref[...]
    o_ref[...] = (o * (g * _sigmoid(g))).astype(o_ref.dtype)


def _softplus2(z):
    return jnp.maximum(z, 0.0) + jnp.log2(1.0 + jnp.exp2(-jnp.abs(z)))


def _stick_weights(z, valid, carry, tri):
    sp = _softplus2(z)
    if valid is not None:
        sp = jnp.where(valid, sp, 0.0)
    pieces = [None] * (z.shape[1] // TRI)
    for sb in range(len(pieces) - 1, -1, -1):
        cols = slice(sb * TRI, (sb + 1) * TRI)
        sps = sp[:, cols]
        suffix = jnp.dot(sps.astype(BF16), tri, preferred_element_type=F32)
        pieces[sb] = jnp.exp2(jnp.minimum(z[:, cols] - suffix, 0.0) - carry)
        carry = carry + suffix[:, :1]
    a = pieces[0] if len(pieces) == 1 else jnp.concatenate(pieces, axis=1)
    if valid is not None:
        a = jnp.where(valid, a, 0.0)
    return a, carry


def _kv_tiles(ref, kj, bk):
    per = bk // KV_TILE
    tiles = [ref[kj * per + i] for i in range(per)]
    return tiles[0] if per == 1 else jnp.concatenate(tiles, axis=1)


def _attn_a_prompt_body(lam_ref, subg_ref, q_ref, kt_ref, v_ref, g_ref, o_ref,
                        qz_ref, m_ref, l_ref, acc_ref, *, bq, bk, diag_slab, lam_init):
    qi = pl.program_id(2)
    q1, q2 = _halves(q_ref[...])
    qz_ref[0] = q1
    qz_ref[1] = q2
    m_ref[...] = jnp.full(m_ref.shape, NEG_INF, F32)
    l_ref[...] = jnp.zeros(l_ref.shape, F32)
    acc_ref[...] = jnp.zeros(acc_ref.shape, F32)

    def step(kj, diag):
        kt = _kv_tiles(kt_ref, kj, bk)
        v = v_ref[pl.ds(pl.multiple_of(kj * bk, bk), bk), :]
        if diag is not None:
            v = jnp.concatenate([v, jnp.ones_like(v)], axis=1)
        slab = bq if diag is None else diag_slab
        for r0 in range(0, bq, slab):
            if diag is None or _chunk_of((diag + 1) * bk - 1) <= _chunk_of(r0):
                vis = None
            elif _chunk_of(diag * bk) > _chunk_of(r0 + slab - 1):
                continue
            else:
                qpos = r0 + lax.broadcasted_iota(jnp.int32, (slab, bk), 0)
                kpos = diag * bk + lax.broadcasted_iota(jnp.int32, (slab, bk), 1)
                vis = _chunk_of(kpos) <= _chunk_of(qpos)
            rows = slice(r0, r0 + slab)
            for c in range(2):
                s = jnp.dot(qz_ref[c, rows, :], kt, preferred_element_type=F32)
                if vis is not None:
                    s = jnp.where(vis, s, NEG_INF)
                m_prev = m_ref[c, rows, :]
                m_new = jnp.maximum(m_prev, jnp.max(s, axis=-1, keepdims=True))
                alpha = jnp.exp2(m_prev - m_new)
                p = jnp.exp2(s - jnp.tile(m_new, (1, bk // LANES)))
                pv = jnp.dot(p.astype(BF16), v, preferred_element_type=F32)
                psum = jnp.sum(p, axis=-1, keepdims=True) if diag is None else pv[:, LANES:]
                l_ref[c, rows, :] = alpha * l_ref[c, rows, :] + psum
                acc_ref[c, rows, :] = alpha * acc_ref[c, rows, :] + pv[:, :LANES]
                m_ref[c, rows, :] = m_new

    n_full = qi * (bq // bk)

    def body(kj, c):
        step(kj, None)
        return c

    lax.fori_loop(0, n_full, body, 0)
    for r in range(bq // bk):
        step(n_full + r, r)

    _diff_epilogue(acc_ref[0] / l_ref[0], acc_ref[1] / l_ref[1],
                   lam_ref, subg_ref, g_ref, o_ref, lam_init)


def _attn_a_prompt(q, ktb, vb, gate, lam4, subg, lam_init):
    b, t, _ = q.shape
    bq, bk = A_BQ, A_BK
    assert bq % bk == 0 and bk % KV_TILE == 0 and t % bq == 0 and bk % CHUNK == 0
    q_spec = pl.BlockSpec((None, bq, LANES), lambda bi, h, qi: (bi, qi, h))
    kt_spec = pl.BlockSpec((None, t // KV_TILE, LANES, KV_TILE), lambda bi, h, qi: (bi, 0, h, 0))
    v_spec = pl.BlockSpec((None, t, LANES), lambda bi, h, qi: (bi, 0, h))
    full = lambda shape: pl.BlockSpec(shape, lambda bi, h, qi: (0,) * len(shape))
    return pl.pallas_call(
        functools.partial(_attn_a_prompt_body, bq=bq, bk=bk, diag_slab=A_SLAB, lam_init=lam_init),
        grid=(b, A_HEADS, t // bq),
        in_specs=[full(lam4.shape), full(subg.shape), q_spec, kt_spec, v_spec, q_spec],
        out_specs=q_spec,
        out_shape=jax.ShapeDtypeStruct((b, t, WIDTH), BF16),
        scratch_shapes=[pltpu.VMEM((2, bq, LANES), BF16),
                        pltpu.VMEM((2, bq, LANES), F32),
                        pltpu.VMEM((2, bq, LANES), F32),
                        pltpu.VMEM((2, bq, LANES), F32)],
        compiler_params=_params(3),
        name="attn_a_prompt",
    )(lam4, subg, q, ktb, vb, gate)


def _attn_b_prompt_body(tri_ref, q_ref, kt_ref, vt_ref, g_ref, o_ref,
                        qz_ref, carry_ref, acc_ref, *, bq, sub):
    qi = pl.program_id(2)
    q1, q2 = _halves(q_ref[...])
    qz_ref[0] = q1
    qz_ref[1] = q2
    carry_ref[...] = jnp.zeros(carry_ref.shape, F32)
    acc_ref[...] = jnp.zeros(acc_ref.shape, F32)

    def fold(rows, kt, vt, valid):
        n_sb = kt.shape[1] // TRI
        zs = [jnp.dot(qz_ref[c, rows, :], kt, preferred_element_type=F32) for c in range(2)]
        sps = [_softplus2(z) for z in zs]
        if valid is not None:
            sps = [jnp.where(valid, sp, 0.0) for sp in sps]
        carries = [carry_ref[c, rows, :] for c in range(2)]
        pieces = [[None] * n_sb for _ in range(2)]
        for sb in range(n_sb - 1, -1, -1):
            cols = slice(sb * TRI, (sb + 1) * TRI)
            sfx = [jnp.dot(sp[:, cols].astype(BF16), tri_ref[...], preferred_element_type=F32) for sp in sps]
            for c in range(2):
                pieces[c][sb] = jnp.exp2(jnp.minimum(zs[c][:, cols] - sfx[c], 0.0) - carries[c])
                carries[c] = carries[c] + sfx[c][:, :1]
        for c in range(2):
            a = pieces[c][0] if n_sb == 1 else jnp.concatenate(pieces[c], axis=1)
            if valid is not None:
                a = jnp.where(valid, a, 0.0)
            acc_ref[c, rows, :] += lax.dot_general(a.astype(BF16), vt, _NT, preferred_element_type=F32)
            carry_ref[c, rows, :] = carries[c]

    assert sub == KV_TILE
    row = lax.broadcasted_iota(jnp.int32, (sub, 2 * sub), 0)
    col = lax.broadcasted_iota(jnp.int32, (sub, 2 * sub), 1)
    causal = col < row + sub
    for j in range(bq // sub):
        rows = slice(j * sub, (j + 1) * sub)
        own = qi * (bq // sub) + j
        prev = jnp.maximum(own - 1, 0)
        kt = jnp.concatenate([kt_ref[prev], kt_ref[own]], axis=1)
        vt = jnp.concatenate([vt_ref[prev], vt_ref[own]], axis=1)
        valid = jnp.logical_and(causal, jnp.logical_or(col >= sub, own > 0))
        fold(rows, kt, vt, valid)

    for j in range(bq // sub):
        rows = slice(j * sub, (j + 1) * sub)
        own = qi * (bq // sub) + j

        def exhausted(rows=rows):
            return jnp.min(carry_ref[:, rows, :]) >= STICK_EXIT

        def cond(state):
            t, done = state
            return jnp.logical_and(t >= 0, jnp.logical_not(done))

        def body(state, rows=rows, exhausted=exhausted):
            t, _ = state
            fold(rows, kt_ref[t], vt_ref[t], None)
            return t - 1, exhausted()

        lax.while_loop(cond, body, (own - 2, exhausted()))

    g = g_ref[...]
    o = jnp.where(_lo_lanes(), acc_ref[0], acc_ref[1])
    o_ref[...] = (o * (g * _sigmoid(g))).astype(o_ref.dtype)


def _attn_b_prompt(q, ktb, vtb, gate, tri):
    b, t, _ = q.shape
    bq = B_BQ
    assert bq % KV_TILE == 0 and t % bq == 0 and KV_TILE % TRI == 0
    q_spec = pl.BlockSpec((None, bq, LANES), lambda bi, h, qi: (bi, qi, h))
    kt_spec = pl.BlockSpec((None, t // KV_TILE, LANES, KV_TILE), lambda bi, h, qi: (bi, 0, h, 0))
    return pl.pallas_call(
        functools.partial(_attn_b_prompt_body, bq=bq, sub=KV_TILE),
        grid=(b, WIDTH // LANES, t // bq),
        in_specs=[pl.BlockSpec((TRI, TRI), lambda bi, h, qi: (0, 0)),
                  q_spec, kt_spec, kt_spec, q_spec],
        out_specs=q_spec,
        out_shape=jax.ShapeDtypeStruct((b, t, WIDTH), BF16),
        scratch_shapes=[pltpu.VMEM((2, bq, LANES), BF16),
                        pltpu.VMEM((2, bq, 1), F32),
                        pltpu.VMEM((2, bq, LANES), F32)],
        compiler_params=_params(3),
        name="attn_b_prompt",
    )(tri, q, ktb, vtb, gate)


def _pad_rows(x, rows):
    return jnp.concatenate([x, jnp.zeros((rows - x.shape[0], x.shape[1]), x.dtype)], axis=0)


def _attn_a_sample_body(lam_ref, subg_ref, q_ref, kn_ref, vn_ref, ktp_ref, vp_ref, g_ref, o_ref,
                        qq_ref, m_ref, l_ref, acc_ref, *, t_new, past, chunk, lam_init):
    pc = pl.program_id(1)

    @pl.when(pc == 0)
    def _():
        for h in range(A_HEADS):
            qq_ref[h] = jnp.concatenate(_halves(q_ref[:, h * LANES:(h + 1) * LANES]), axis=0)
        m_ref[...] = jnp.full(m_ref.shape, NEG_INF, F32)
        l_ref[...] = jnp.zeros(l_ref.shape, F32)
        acc_ref[...] = jnp.zeros(acc_ref.shape, F32)

    def update(h, s, v):
        m_prev = m_ref[h]
        m_new = jnp.maximum(m_prev, jnp.max(s, axis=-1, keepdims=True))
        alpha = jnp.exp2(m_prev - m_new)
        p = jnp.exp2(s - jnp.tile(m_new, (1, s.shape[1] // LANES)))
        l_ref[h] = alpha * l_ref[h] + jnp.sum(p, axis=-1, keepdims=True)
        acc_ref[h] = alpha * acc_ref[h] + jnp.dot(p.astype(BF16), v, preferred_element_type=F32)
        m_ref[h] = m_new

    for h in range(A_HEADS):
        kt = ktp_ref[h * LANES:(h + 1) * LANES, :].astype(BF16)
        v = vp_ref[pl.ds(h, chunk, stride=A_HEADS), :].astype(BF16)
        update(h, jnp.dot(qq_ref[h], kt, preferred_element_type=F32), v)

    @pl.when(pc == pl.num_programs(1) - 1)
    def _():
        row = lax.broadcasted_iota(jnp.int32, (2 * t_new, LANES), 0)
        col = lax.broadcasted_iota(jnp.int32, (2 * t_new, LANES), 1)
        vis = jnp.logical_and(col < t_new,
                              _chunk_of(past + col) <= _chunk_of(past + _mod_pow2(row, t_new)))
        for h in range(A_HEADS):
            cols = slice(h * LANES, (h + 1) * LANES)
            kn = _pad_rows(kn_ref[:, cols], LANES)
            vn = _pad_rows(vn_ref[:, cols], LANES)
            s_n = lax.dot_general(qq_ref[h], kn, _NT, preferred_element_type=F32)
            update(h, jnp.where(vis, s_n, NEG_INF), vn)
            o = acc_ref[h] / l_ref[h]
            _diff_epilogue(o[:t_new], o[t_new:], lam_ref, subg_ref, g_ref.at[:, cols], o_ref.at[:, cols],
                           lam_init)


def _attn_a_sample(q, kb, vb, kt_past, v_past, gate, lam4, subg, lam_init):
    b, t_new, _ = q.shape
    past = kt_past.shape[2]
    chunk = SAMPLE_CHUNK
    assert v_past.shape[1] == past * A_HEADS and past % chunk == 0
    new_spec = pl.BlockSpec((None, t_new, WIDTH), lambda bi, pc: (bi, 0, 0))
    ktp_spec = pl.BlockSpec((None, WIDTH, chunk), lambda bi, pc: (bi, 0, pc))
    vp_spec = pl.BlockSpec((None, chunk * A_HEADS, LANES), lambda bi, pc: (bi, pc, 0))
    full = lambda shape: pl.BlockSpec(shape, lambda bi, pc: (0,) * len(shape))
    state = pltpu.VMEM((A_HEADS, 2 * t_new, LANES), F32)
    return pl.pallas_call(
        functools.partial(_attn_a_sample_body, t_new=t_new, past=past, chunk=chunk, lam_init=lam_init),
        grid=(b, past // chunk),
        in_specs=[full(lam4.shape), full(subg.shape), new_spec, new_spec, new_spec,
                  ktp_spec, vp_spec, new_spec],
        out_specs=new_spec,
        out_shape=jax.ShapeDtypeStruct((b, t_new, WIDTH), BF16),
        scratch_shapes=[pltpu.VMEM((A_HEADS, 2 * t_new, LANES), BF16), state, state, state],
        compiler_params=_params(2),
        name="attn_a_sample",
    )(lam4, subg, q, kb, vb, kt_past, v_past, gate)


def _attn_b_sample_body(tri_ref, q_ref, kn_ref, vn_ref, g_ref, kt_hbm, vt_hbm, o_ref,
                        kbuf, vbuf, sems, qq_ref, carry_ref, acc_ref, *, t_new, past, chunk):
    bi = pl.program_id(0)
    rows = 2 * t_new
    n_blk = chunk // TRI
    n_chunks = past // chunk
    n_pairs = WIDTH // LANES
    slot = lax.rem(bi, 2)
    tri = tri_ref[...]

    def fetch(batch, ci, into):
        cols = pl.ds(pl.multiple_of(ci * chunk, chunk), chunk)
        return (pltpu.make_async_copy(kt_hbm.at[batch, :, cols], kbuf.at[into], sems.at[0, into]),
                pltpu.make_async_copy(vt_hbm.at[batch, :, cols], vbuf.at[into], sems.at[1, into]))

    @pl.when(bi == 0)
    def _():
        for cp in fetch(0, n_chunks - 1, 0):
            cp.start()

    @pl.when(bi + 1 < pl.num_programs(0))
    def _():
        for cp in fetch(bi + 1, n_chunks - 1, 1 - slot):
            cp.start()

    row = lax.broadcasted_iota(jnp.int32, (rows, TRI), 0)
    col = lax.broadcasted_iota(jnp.int32, (rows, TRI), 1)
    valid = col < _mod_pow2(row, t_new)
    for hp in range(n_pairs):
        cols = slice(hp * LANES, (hp + 1) * LANES)
        qq = jnp.concatenate(_halves(q_ref[:, cols]), axis=0)
        qq_ref[hp] = qq
        kn = _pad_rows(kn_ref[:, cols], TRI)
        vn = _pad_rows(vn_ref[:, cols], TRI)
        z_n = lax.dot_general(qq, kn, _NT, preferred_element_type=F32)
        a_n, carry = _stick_weights(z_n, valid, jnp.zeros((rows, 1), F32), tri)
        acc_ref[hp] = jnp.dot(a_n.astype(BF16), vn, preferred_element_type=F32)
        carry_ref[hp] = carry

    def absorb():
        for hp in range(n_pairs):
            feat = pl.ds(hp * LANES, LANES)
            z_all = jnp.dot(qq_ref[hp], kbuf[slot, feat, :].astype(BF16), preferred_element_type=F32)
            z = jnp.concatenate([z_all[:, c * TRI:(c + 1) * TRI] for c in range(n_blk)], axis=0)
            suffix = jnp.dot(_softplus2(z).astype(BF16), tri, preferred_element_type=F32)
            running = carry_ref[hp]
            carries = [None] * n_blk
            for c in range(n_blk - 1, -1, -1):
                carries[c] = running
                running = running + suffix[c * rows:(c + 1) * rows, :1]
            carry_ref[hp] = running
            a = jnp.exp2(jnp.minimum(z - suffix, 0.0) - jnp.concatenate(carries, axis=0)).astype(BF16)
            vtp = vbuf[slot, feat, :].astype(BF16)
            acc = acc_ref[hp]
            for c in range(n_blk):
                acc = acc + lax.dot_general(a[c * rows:(c + 1) * rows], vtp[:, c * TRI:(c + 1) * TRI], _NT,
                                            preferred_element_type=F32)
            acc_ref[hp] = acc

    def exhausted():
        return jnp.min(carry_ref[...]) >= STICK_EXIT

    for cp in fetch(bi, n_chunks - 1, slot):
        cp.wait()
    absorb()

    def cond(state):
        ci, done = state
        return jnp.logical_and(ci >= 0, jnp.logical_not(done))

    def body(state):
        ci, _ = state
        copies = fetch(bi, ci, slot)
        for cp in copies:
            cp.start()
        for cp in copies:
            cp.wait()
        absorb()
        return ci - 1, exhausted()

    lax.while_loop(cond, body, (jnp.int32(n_chunks - 2), exhausted()))

    for hp in range(n_pairs):
        cols = slice(hp * LANES, (hp + 1) * LANES)
        acc = acc_ref[hp]
        g = g_ref[:, cols]
        o = jnp.where(_lo_lanes(), acc[:t_new], acc[t_new:])
        o_ref[:, cols] = (o * (g * _sigmoid(g))).astype(o_ref.dtype)


def _attn_b_sample(q, kb, vb, kt_past, vt_past, gate, tri):
    b, t_new, _ = q.shape
    past = kt_past.shape[2]
    chunk = SAMPLE_B_CHUNK
    assert past % chunk == 0 and chunk % TRI == 0 and t_new <= TRI
    n_pairs = WIDTH // LANES
    new_spec = pl.BlockSpec((None, t_new, WIDTH), lambda bi: (bi, 0, 0))
    hbm_spec = pl.BlockSpec(memory_space=pl.ANY)
    return pl.pallas_call(
        functools.partial(_attn_b_sample_body, t_new=t_new, past=past, chunk=chunk),
        grid=(b,),
        in_specs=[pl.BlockSpec((TRI, TRI), lambda bi: (0, 0)),
                  new_spec, new_spec, new_spec, new_spec, hbm_spec, hbm_spec],
        out_specs=new_spec,
        out_shape=jax.ShapeDtypeStruct((b, t_new, WIDTH), BF16),
        scratch_shapes=[pltpu.VMEM((2, WIDTH, chunk), F32),
                        pltpu.VMEM((2, WIDTH, chunk), F32),
                        pltpu.SemaphoreType.DMA((2, 2)),
                        pltpu.VMEM((n_pairs, 2 * t_new, LANES), BF16),
                        pltpu.VMEM((n_pairs, 2 * t_new, 1), F32),
                        pltpu.VMEM((n_pairs, 2 * t_new, LANES), F32)],
        compiler_params=_params(1),
        name="attn_b_sample",
    )(tri, q, kb, vb, gate, kt_past, vt_past)


def _post_body(*refs, final):
    if final:
        o_ref, x_ref, p_ref, wo_ref, wg_ref, wp_ref, fg_ref, out_ref = refs
    else:
        o_ref, x_ref, p_ref, wo_ref, wg_ref, wp_ref, out_ref = refs
    x = x_ref[...] + jnp.dot(o_ref[...], wo_ref[...], preferred_element_type=F32)
    gate = _sigmoid(jnp.dot(x.astype(BF16), wg_ref[...], preferred_element_type=F32))
    x = x + gate * jnp.dot(p_ref[...].astype(BF16), wp_ref[...], preferred_element_type=F32)
    if final:
        ms = jnp.mean(x * x, axis=-1, keepdims=True)
        x = (x * lax.rsqrt(ms + EPS)) * fg_ref[...]
    out_ref[...] = x


def _post(o, x, p, layer, w_out, w_gate, w_proj, final_g):
    n = x.shape[0]
    rows = POST_ROWS
    assert n % rows == 0
    final = final_g is not None
    row_spec = lambda width: pl.BlockSpec((rows, width), lambda i: (i, 0))
    full = lambda shape: pl.BlockSpec(shape, lambda i: (0, 0))
    in_specs = [row_spec(WIDTH), row_spec(D_MODEL),
                pl.BlockSpec((None, rows, PLE_DIM), lambda i: (layer, i, 0)),
                full(w_out.shape), full(w_gate.shape), full(w_proj.shape)]
    args = [o, x, p, w_out, w_gate, w_proj]
    if final:
        in_specs.append(full((1, D_MODEL)))
        args.append(final_g.reshape(1, D_MODEL))
    return pl.pallas_call(
        functools.partial(_post_body, final=final),
        grid=(n // rows,),
        in_specs=in_specs,
        out_specs=row_spec(D_MODEL),
        out_shape=jax.ShapeDtypeStruct((n, D_MODEL), F32),
        compiler_params=_params(1),
        name="post_final" if final else "post",
    )(*args)


def _mixer_order(depth):
    return [("a", i // 2) if i % 2 == 0 else ("b", i // 2) for i in range(depth)]


def _lam_init(i):
    return 0.8 - 0.6 * math.exp(-0.3 * i)


def _trunk_prompt(x, p, pos, weights):
    (a_norm_g, a_w_in, a_lam, a_subln_g, a_w_out, b_norm_g, b_w_in, b_w_out,
     ple_w_proj, ple_w_gate, final_norm_g) = weights
    b, t, _ = x.shape
    n = b * t
    depth = p.shape[0]
    xf = x.reshape(n, D_MODEL)
    tables = _rope_tables(pos, PRE_ROWS)
    tri = jnp.tri(TRI, dtype=BF16)
    shp = (b, t, WIDTH)
    ak, av, bk, bv = [], [], [], []
    for i, (kind, j) in enumerate(_mixer_order(depth)):
        if kind == "a":
            q, kt, ktb, v, vb, gate = _pre_cols(xf, b, t, a_norm_g[j], a_w_in[j], tables, False)
            o = _attn_a_prompt(q.reshape(shp), ktb, vb.reshape(shp), gate.reshape(shp),
                               a_lam[j], a_subln_g[j].reshape(1, 2 * A_DH), _lam_init(i))
            ak.append(kt.reshape(b, A_HEADS, 2, A_DH, t).transpose(0, 4, 1, 2, 3))
            av.append(v.reshape(b, t, A_HEADS, 2 * A_DH))
            w_out = a_w_out[j]
        else:
            q, kt, ktb, vt, vtb, gate = _pre_cols(xf, b, t, b_norm_g[j], b_w_in[j], None, True)
            o = _attn_b_prompt(q.reshape(shp), ktb, vtb, gate.reshape(shp), tri)
            bk.append(kt.reshape(b, B_HEADS, B_DH, t).transpose(0, 3, 1, 2))
            bv.append(vt.reshape(b, B_HEADS, B_DH, t).transpose(0, 3, 1, 2))
            w_out = b_w_out[j]
        xf = _post(o.reshape(n, WIDTH), xf, p.reshape(depth, n, PLE_DIM), i, w_out,
                   ple_w_gate[i], ple_w_proj[i], final_norm_g if i == depth - 1 else None)
    return (xf.reshape(b, t, D_MODEL), jnp.stack(ak), jnp.stack(av), jnp.stack(bk), jnp.stack(bv))


def _trunk_sample(x, p, pos, weights, caches):
    (a_norm_g, a_w_in, a_lam, a_subln_g, a_w_out, b_norm_g, b_w_in, b_w_out,
     ple_w_proj, ple_w_gate, final_norm_g) = weights
    cache_a_k, cache_a_v, cache_b_k, cache_b_v = caches
    b, t, _ = x.shape
    n = b * t
    depth = p.shape[0]
    past = cache_a_k.shape[2]
    xf = x.reshape(n, D_MODEL)
    lane_tables, _ = _rope_tables(pos, PRE_ROWS)
    tri = jnp.tri(TRI, dtype=BF16)
    shp = (b, t, WIDTH)
    ak, av, bk, bv = [], [], [], []
    for i, (kind, j) in enumerate(_mixer_order(depth)):
        if kind == "a":
            q, k, kb, v, vb, gate = _pre_rows(xf, a_norm_g[j], a_w_in[j], lane_tables)
            kt_past = cache_a_k[j].transpose(0, 2, 3, 4, 1).reshape(b, WIDTH, past)
            v_past = cache_a_v[j].reshape(b, past * A_HEADS, 2 * A_DH)
            o = _attn_a_sample(q.reshape(shp), kb.reshape(shp), vb.reshape(shp), kt_past, v_past,
                               gate.reshape(shp), a_lam[j], a_subln_g[j].reshape(1, 2 * A_DH), _lam_init(i))
            ak.append(k.reshape(b, t, A_HEADS, 2, A_DH))
            av.append(v.reshape(b, t, A_HEADS, 2 * A_DH))
            w_out = a_w_out[j]
        else:
            q, k, kb, v, vb, gate = _pre_rows(xf, b_norm_g[j], b_w_in[j], None)
            kt_past = cache_b_k[j].transpose(0, 2, 3, 1).reshape(b, WIDTH, past)
            vt_past = cache_b_v[j].transpose(0, 2, 3, 1).reshape(b, WIDTH, past)
            o = _attn_b_sample(q.reshape(shp), kb.reshape(shp), vb.reshape(shp), kt_past, vt_past,
                               gate.reshape(shp), tri)
            bk.append(k.reshape(b, t, B_HEADS, B_DH))
            bv.append(v.reshape(b, t, B_HEADS, B_DH))
            w_out = b_w_out[j]
        xf = _post(o.reshape(n, WIDTH), xf, p.reshape(depth, n, PLE_DIM), i, w_out,
                   ple_w_gate[i], ple_w_proj[i], final_norm_g if i == depth - 1 else None)
    return (xf.reshape(b, t, D_MODEL), jnp.stack(ak), jnp.stack(av), jnp.stack(bk), jnp.stack(bv))


def kernel(x_prompt, x_sample, cache_a_k, cache_a_v, cache_b_k, cache_b_v, p_prompt, p_sample,
           a_norm_g, a_w_in, a_lam_q1, a_lam_k1, a_lam_q2, a_lam_k2, a_subln_g, a_w_out,
           b_norm_g, b_w_in, b_w_out, ple_w_proj, ple_w_gate, final_norm_g):
    past = cache_a_k.shape[2]
    a_lam = jnp.stack([a_lam_q1, a_lam_k1, a_lam_q2, a_lam_k2], axis=1)
    weights = (a_norm_g, a_w_in.astype(BF16), a_lam, a_subln_g, a_w_out.astype(BF16),
               b_norm_g, b_w_in.astype(BF16), b_w_out.astype(BF16),
               ple_w_proj.astype(BF16), ple_w_gate.astype(BF16), final_norm_g)
    pos_p = jnp.arange(x_prompt.shape[1], dtype=jnp.int32)
    pos_s = past + jnp.arange(x_sample.shape[1], dtype=jnp.int32)
    y_s, ak_s, av_s, bk_s, bv_s = _trunk_sample(x_sample, p_sample, pos_s, weights,
                                                (cache_a_k, cache_a_v, cache_b_k, cache_b_v))
    y_p, ak_p, av_p, bk_p, bv_p = _trunk_prompt(x_prompt, p_prompt, pos_p, weights)
    return (y_p, y_s, ak_p, av_p, bk_p, bv_p, ak_s, av_s, bk_s, bv_s)
```

```python
import functools
import math

import jax
import jax.numpy as jnp
from jax import lax
from jax.experimental import pallas as pl
from jax.experimental.pallas import tpu as pltpu

F32 = jnp.float32
BF16 = jnp.bfloat16

D_MODEL = 1024
CHUNK = 64
PLE_DIM = 256
ROPE_THETA = 500000.0
EPS = 1e-6
NEG_INF = -1e30
A_HEADS = 8
A_DH = 64
A_ROT = A_DH // 4
B_HEADS = 16
B_DH = 64
WIDTH = 1024
LANES = 128
SUBLANES = 8
HALF = 64
Q_SCALE = A_DH ** -0.5 * math.log2(math.e)

PRE_ROWS = 256
PRE_COLS_ROWS = 512
POST_ROWS = 512
KV_TILE = 256
TRI = 256
A_BQ, A_BK = 2048, 512
B_BQ = 2048
A_SLAB = 512
STICK_EXIT = 151.0
SAMPLE_B_CHUNK = 256
SAMPLE_CHUNK = 2048
VMEM_LIMIT = 48 * 1024 * 1024

_NT = (((1,), (1,)), ((), ()))


def _params(n_axes, vmem_limit=VMEM_LIMIT):
    return pltpu.CompilerParams(
        dimension_semantics=("arbitrary",) * n_axes, vmem_limit_bytes=vmem_limit)


def _sigmoid(x):
    return 1.0 / (1.0 + jnp.exp(-x))


def _mod_pow2(x, n):
    assert n & (n - 1) == 0
    return x & (n - 1)


def _chunk_of(pos):
    assert CHUNK & (CHUNK - 1) == 0
    return pos >> (CHUNK.bit_length() - 1)


def _rope_lanes(sl, c_ref, s1_ref, s2_ref):
    return (sl * c_ref[...] + pltpu.roll(sl, LANES - A_ROT // 2, 1) * s1_ref[...]
            + pltpu.roll(sl, A_ROT // 2, 1) * s2_ref[...])


def _normed(x_ref, g_ref):
    x = x_ref[...]
    ms = jnp.mean(x * x, axis=-1, keepdims=True)
    return ((x * lax.rsqrt(ms + EPS)) * g_ref[...]).astype(BF16)


def _pre_rows_body(*refs, rope):
    if rope:
        x_ref, g_ref, wq_ref, wk_ref, wv_ref, wg_ref, c_ref, s1_ref, s2_ref = refs[:9]
        outs = refs[9:]
    else:
        x_ref, g_ref, wq_ref, wk_ref, wv_ref, wg_ref = refs[:6]
        outs = refs[6:]
    q_ref, k_ref, kb_ref, v_ref, vb_ref, gate_ref = outs
    hb = _normed(x_ref, g_ref)

    def rot(r, j):
        sl = r[:, j * LANES:(j + 1) * LANES]
        return _rope_lanes(sl, c_ref, s1_ref, s2_ref) if rope else sl

    q = jnp.dot(hb, wq_ref[...], preferred_element_type=F32)
    for j in range(WIDTH // LANES):
        q_ref[:, j * LANES:(j + 1) * LANES] = (rot(q, j) * Q_SCALE).astype(BF16)
    k = jnp.dot(hb, wk_ref[...], preferred_element_type=F32)
    for j in range(WIDTH // LANES):
        kr = rot(k, j)
        k_ref[:, j * LANES:(j + 1) * LANES] = kr
        kb_ref[:, j * LANES:(j + 1) * LANES] = kr.astype(BF16)
    v = jnp.dot(hb, wv_ref[...], preferred_element_type=F32)
    v_ref[...] = v
    vb_ref[...] = v.astype(BF16)
    gate_ref[...] = jnp.dot(hb, wg_ref[...], preferred_element_type=F32)


def _pre_cols_body(*refs, rope, v_cols):
    if rope:
        x_ref, g_ref, wq_ref, wkt_ref, wv_ref, wg_ref, c_ref, s1_ref, s2_ref, ct_ref, st_ref = refs[:11]
        outs = refs[11:]
    else:
        x_ref, g_ref, wq_ref, wkt_ref, wv_ref, wg_ref = refs[:6]
        outs = refs[6:]
    q_ref, kt_ref, ktb_ref, v_ref, vb_ref, gate_ref = outs
    hb = _normed(x_ref, g_ref)

    q = jnp.dot(hb, wq_ref[...], preferred_element_type=F32)
    for j in range(WIDTH // LANES):
        sl = q[:, j * LANES:(j + 1) * LANES]
        if rope:
            sl = _rope_lanes(sl, c_ref, s1_ref, s2_ref)
        q_ref[:, j * LANES:(j + 1) * LANES] = (sl * Q_SCALE).astype(BF16)

    def put_tiles(tiles_ref, val):
        for i in range(tiles_ref.shape[0]):
            tiles_ref[i] = val[:, i * KV_TILE:(i + 1) * KV_TILE].astype(BF16)

    kt = lax.dot_general(wkt_ref[...], hb, _NT, preferred_element_type=F32)
    if rope:
        half = A_ROT // 2
        assert half == SUBLANES
        cos, sin = ct_ref[...], st_ref[...]
        parts = []
        for grp in range(WIDTH // A_DH):
            r0 = grp * A_DH
            x0, x1 = kt[r0:r0 + half], kt[r0 + half:r0 + 2 * half]
            parts += [x0 * cos - x1 * sin, x1 * cos + x0 * sin, kt[r0 + 2 * half:r0 + A_DH]]
        kt = jnp.concatenate(parts, axis=0)
    kt_ref[...] = kt
    put_tiles(ktb_ref, kt)

    if v_cols:
        vt = lax.dot_general(wv_ref[...], hb, _NT, preferred_element_type=F32)
        v_ref[...] = vt
        put_tiles(vb_ref, vt)
    else:
        v = jnp.dot(hb, wv_ref[...], preferred_element_type=F32)
        v_ref[...] = v
        vb_ref[...] = v.astype(BF16)
    gate_ref[...] = jnp.dot(hb, wg_ref[...], preferred_element_type=F32)


def _split_w(w_in, t_k, t_v):
    parts = [w_in[:, i * WIDTH:(i + 1) * WIDTH] for i in range(4)]
    if t_k:
        parts[1] = parts[1].T
    if t_v:
        parts[2] = parts[2].T
    return parts


def _pre_rows(x, norm_g, w_in, tables):
    n = x.shape[0]
    rows = PRE_ROWS
    rope = tables is not None
    row_spec = lambda width: pl.BlockSpec((rows, width), lambda i: (i, 0))
    full = lambda shape: pl.BlockSpec(shape, lambda i: (0, 0))
    in_specs = [row_spec(D_MODEL), full((1, D_MODEL))] + [full((D_MODEL, WIDTH))] * 4
    args = [x, norm_g.reshape(1, D_MODEL)] + _split_w(w_in, False, False)
    if rope:
        in_specs += [full((rows, LANES))] * 3
        args += list(tables)
    out_shape = [jax.ShapeDtypeStruct((n, WIDTH), dt) for dt in (BF16, F32, BF16, F32, BF16, F32)]
    return pl.pallas_call(
        functools.partial(_pre_rows_body, rope=rope),
        grid=(n // rows,),
        in_specs=in_specs,
        out_specs=[row_spec(WIDTH)] * 6,
        out_shape=out_shape,
        compiler_params=_params(1),
        name="pre_rows_rope" if rope else "pre_rows",
    )(*args)


def _pre_cols(x, b, t, norm_g, w_in, tables, v_cols):
    rows = PRE_COLS_ROWS
    assert rows % KV_TILE == 0 and t % rows == 0
    n_t = t // rows
    rope = tables is not None
    row_spec = lambda width: pl.BlockSpec((rows, width), lambda i: (i, 0))
    full = lambda shape: pl.BlockSpec(shape, lambda i: (0, 0))
    colf_spec = pl.BlockSpec((None, WIDTH, rows), lambda i: (i // n_t, 0, i % n_t))
    colb_spec = pl.BlockSpec((None, rows // KV_TILE, WIDTH, KV_TILE), lambda i: (i // n_t, i % n_t, 0, 0))
    in_specs = [row_spec(D_MODEL), full((1, D_MODEL))] + [full((D_MODEL, WIDTH))] * 4
    args = [x, norm_g.reshape(1, D_MODEL)] + _split_w(w_in, True, v_cols)
    if rope:
        lanes_tab, rows_tab = tables
        in_specs += [pl.BlockSpec((rows, LANES), lambda i: (i % n_t, 0))] * 3
        in_specs += [pl.BlockSpec((A_ROT // 2, rows), lambda i: (0, i % n_t))] * 2
        args += list(lanes_tab) + list(rows_tab)
    colf = jax.ShapeDtypeStruct((b, WIDTH, t), F32)
    colb = jax.ShapeDtypeStruct((b, t // KV_TILE, WIDTH, KV_TILE), BF16)
    rowf = jax.ShapeDtypeStruct((b * t, WIDTH), F32)
    rowb = jax.ShapeDtypeStruct((b * t, WIDTH), BF16)
    out_shape = [rowb, colf, colb] + ([colf, colb] if v_cols else [rowf, rowb]) + [rowf]
    out_specs = ([row_spec(WIDTH), colf_spec, colb_spec]
                 + ([colf_spec, colb_spec] if v_cols else [row_spec(WIDTH)] * 2) + [row_spec(WIDTH)])
    return pl.pallas_call(
        functools.partial(_pre_cols_body, rope=rope, v_cols=v_cols),
        grid=(b * n_t,),
        in_specs=in_specs,
        out_specs=out_specs,
        out_shape=out_shape,
        compiler_params=_params(1),
        name="pre_cols_rope" if rope else "pre_cols",
    )(*args)


def _rope_tables(pos, rows):
    half = A_ROT // 2
    inv = ROPE_THETA ** (-jnp.arange(half, dtype=F32) * 2.0 / A_ROT)
    ang = pos.astype(F32)[:, None] * inv[None, :]
    cos, sin = jnp.cos(ang), jnp.sin(ang)
    t = pos.shape[0]
    c = jnp.concatenate([cos, cos, jnp.ones((t, HALF - A_ROT), F32)], axis=1)
    s1 = jnp.concatenate([-sin, jnp.zeros((t, HALF - half), F32)], axis=1)
    s2 = jnp.concatenate([jnp.zeros((t, half), F32), sin, jnp.zeros((t, HALF - A_ROT), F32)], axis=1)
    reps = (max(rows // t, 1), LANES // HALF)
    return tuple(jnp.tile(a, reps) for a in (c, s1, s2)), (cos.T, sin.T)


def _lo_lanes():
    return lax.broadcasted_iota(jnp.int32, (1, LANES), 1) < HALF


def _halves(q):
    lo = _lo_lanes()
    zero = jnp.zeros_like(q)
    return jnp.where(lo, q, zero), jnp.where(lo, zero, q)


def _lam(lam_ref, lam_init):
    a = jnp.sum(lam_ref[0:1, :] * lam_ref[1:2, :], axis=-1, keepdims=True)
    b = jnp.sum(lam_ref[2:3, :] * lam_ref[3:4, :], axis=-1, keepdims=True)
    return jnp.exp(a) - jnp.exp(b) + lam_init


def _diff_epilogue(o1, o2, lam_ref, subg_ref, g_ref, o_ref, lam_init):
    o = o1 - _lam(lam_ref, lam_init) * o2
    ms = jnp.mean(o * o, axis=-1, keepdims=True)
    o = (o * lax.rsqrt(ms + EPS)) * subg_ref[...] * (1.0 - lam_init)
    g = g_ref[...]
    o_ref[...] = (o * (g * _sigmoid(g))).astype(o_ref.dtype)


def _softplus2(z):
    return jnp.maximum(z, 0.0) + jnp.log2(1.0 + jnp.exp2(-jnp.abs(z)))


def _stick_weights(z, valid, carry, tri):
    sp = _softplus2(z)
    if valid is not None:
        sp = jnp.where(valid, sp, 0.0)
    pieces = [None] * (z.shape[1] // TRI)
    for sb in range(len(pieces) - 1, -1, -1):
        cols = slice(sb * TRI, (sb + 1) * TRI)
        sps = sp[:, cols]
        suffix = jnp.dot(sps.astype(BF16), tri, preferred_element_type=F32)
        pieces[sb] = jnp.exp2(jnp.minimum(z[:, cols] - suffix, 0.0) - carry)
        carry = carry + suffix[:, :1]
    a = pieces[0] if len(pieces) == 1 else jnp.concatenate(pieces, axis=1)
    if valid is not None:
        a = jnp.where(valid, a, 0.0)
    return a, carry


def _kv_tiles(ref, kj, bk):
    per = bk // KV_TILE
    tiles = [ref[kj * per + i] for i in range(per)]
    return tiles[0] if per == 1 else jnp.concatenate(tiles, axis=1)


def _attn_a_prompt_body(lam_ref, subg_ref, q_ref, kt_ref, v_ref, g_ref, o_ref,
                        qz_ref, m_ref, l_ref, acc_ref, *, bq, bk, diag_slab, lam_init):
    qi = pl.program_id(2)
    q1, q2 = _halves(q_ref[...])
    qz_ref[0] = q1
    qz_ref[1] = q2
    m_ref[...] = jnp.full(m_ref.shape, NEG_INF, F32)
    l_ref[...] = jnp.zeros(l_ref.shape, F32)
    acc_ref[...] = jnp.zeros(acc_ref.shape, F32)

    def step(kj, diag):
        kt = _kv_tiles(kt_ref, kj, bk)
        v = v_ref[pl.ds(pl.multiple_of(kj * bk, bk), bk), :]
        if diag is not None:
            v = jnp.concatenate([v, jnp.ones_like(v)], axis=1)
        slab = bq if diag is None else diag_slab
        for r0 in range(0, bq, slab):
            if diag is None or _chunk_of((diag + 1) * bk - 1) <= _chunk_of(r0):
                vis = None
            elif _chunk_of(diag * bk) > _chunk_of(r0 + slab - 1):
                continue
            else:
                qpos = r0 + lax.broadcasted_iota(jnp.int32, (slab, bk), 0)
                kpos = diag * bk + lax.broadcasted_iota(jnp.int32, (slab, bk), 1)
                vis = _chunk_of(kpos) <= _chunk_of(qpos)
            rows = slice(r0, r0 + slab)
            for c in range(2):
                s = jnp.dot(qz_ref[c, rows, :], kt, preferred_element_type=F32)
                if vis is not None:
                    s = jnp.where(vis, s, NEG_INF)
                m_prev = m_ref[c, rows, :]
                m_new = jnp.maximum(m_prev, jnp.max(s, axis=-1, keepdims=True))
                alpha = jnp.exp2(m_prev - m_new)
                p = jnp.exp2(s - jnp.tile(m_new, (1, bk // LANES)))
                pv = jnp.dot(p.astype(BF16), v, preferred_element_type=F32)
                psum = jnp.sum(p, axis=-1, keepdims=True) if diag is None else pv[:, LANES:]
                l_ref[c, rows, :] = alpha * l_ref[c, rows, :] + psum
                acc_ref[c, rows, :] = alpha * acc_ref[c, rows, :] + pv[:, :LANES]
                m_ref[c, rows, :] = m_new

    n_full = qi * (bq // bk)

    assert (bq // bk) % 2 == 0

    def body(pair, c):
        step(2 * pair, None)
        step(2 * pair + 1, None)
        return c

    lax.fori_loop(0, n_full // 2, body, 0)
    for r in range(bq // bk):
        step(n_full + r, r)

    _diff_epilogue(acc_ref[0] / l_ref[0], acc_ref[1] / l_ref[1],
                   lam_ref, subg_ref, g_ref, o_ref, lam_init)


def _attn_a_prompt(q, ktb, vb, gate, lam4, subg, lam_init):
    b, t, _ = q.shape
    bq, bk = A_BQ, A_BK
    assert bq % bk == 0 and bk % KV_TILE == 0 and t % bq == 0 and bk % CHUNK == 0
    q_spec = pl.BlockSpec((None, bq, LANES), lambda bi, h, qi: (bi, qi, h))
    kt_spec = pl.BlockSpec((None, t // KV_TILE, LANES, KV_TILE), lambda bi, h, qi: (bi, 0, h, 0))
    v_spec = pl.BlockSpec((None, t, LANES), lambda bi, h, qi: (bi, 0, h))
    full = lambda shape: pl.BlockSpec(shape, lambda bi, h, qi: (0,) * len(shape))
    return pl.pallas_call(
        functools.partial(_attn_a_prompt_body, bq=bq, bk=bk, diag_slab=A_SLAB, lam_init=lam_init),
        grid=(b, A_HEADS, t // bq),
        in_specs=[full(lam4.shape), full(subg.shape), q_spec, kt_spec, v_spec, q_spec],
        out_specs=q_spec,
        out_shape=jax.ShapeDtypeStruct((b, t, WIDTH), BF16),
        scratch_shapes=[pltpu.VMEM((2, bq, LANES), BF16),
                        pltpu.VMEM((2, bq, LANES), F32),
                        pltpu.VMEM((2, bq, LANES), F32),
                        pltpu.VMEM((2, bq, LANES), F32)],
        compiler_params=_params(3),
        name="attn_a_prompt",
    )(lam4, subg, q, ktb, vb, gate)


def _attn_b_prompt_body(tri_ref, q_ref, kt_ref, vt_ref, g_ref, o_ref,
                        qz_ref, carry_ref, acc_ref, *, bq, sub):
    qi = pl.program_id(2)
    q1, q2 = _halves(q_ref[...])
    qz_ref[0] = q1
    qz_ref[1] = q2
    carry_ref[...] = jnp.zeros(carry_ref.shape, F32)
    acc_ref[...] = jnp.zeros(acc_ref.shape, F32)

    def fold(rows, kt, vt, valid):
        n_sb = kt.shape[1] // TRI
        zs = [jnp.dot(qz_ref[c, rows, :], kt, preferred_element_type=F32) for c in range(2)]
        sps = [_softplus2(z) for z in zs]
        if valid is not None:
            sps = [jnp.where(valid, sp, 0.0) for sp in sps]
        carries = [carry_ref[c, rows, :] for c in range(2)]
        pieces = [[None] * n_sb for _ in range(2)]
        for sb in range(n_sb - 1, -1, -1):
            cols = slice(sb * TRI, (sb + 1) * TRI)
            sfx = [jnp.dot(sp[:, cols].astype(BF16), tri_ref[...], preferred_element_type=F32) for sp in sps]
            for c in range(2):
                pieces[c][sb] = jnp.exp2(jnp.minimum(zs[c][:, cols] - sfx[c], 0.0) - carries[c])
                carries[c] = carries[c] + sfx[c][:, :1]
        for c in range(2):
            a = pieces[c][0] if n_sb == 1 else jnp.concatenate(pieces[c], axis=1)
            if valid is not None:
                a = jnp.where(valid, a, 0.0)
            acc_ref[c, rows, :] += lax.dot_general(a.astype(BF16), vt, _NT, preferred_element_type=F32)
            carry_ref[c, rows, :] = carries[c]

    assert sub == KV_TILE
    row = lax.broadcasted_iota(jnp.int32, (sub, 2 * sub), 0)
    col = lax.broadcasted_iota(jnp.int32, (sub, 2 * sub), 1)
    causal = col < row + sub
    for j in range(bq // sub):
        rows = slice(j * sub, (j + 1) * sub)
        own = qi * (bq // sub) + j
        prev = jnp.maximum(own - 1, 0)
        kt = jnp.concatenate([kt_ref[prev], kt_ref[own]], axis=1)
        vt = jnp.concatenate([vt_ref[prev], vt_ref[own]], axis=1)
        valid = jnp.logical_and(causal, jnp.logical_or(col >= sub, own > 0))
        fold(rows, kt, vt, valid)

    for j in range(bq // sub):
        rows = slice(j * sub, (j + 1) * sub)
        own = qi * (bq // sub) + j

        def exhausted(rows=rows):
            return jnp.min(carry_ref[:, rows, :]) >= STICK_EXIT

        def cond(state):
            t, done = state
            return jnp.logical_and(t >= 0, jnp.logical_not(done))

        def body(state, rows=rows, exhausted=exhausted):
            t, _ = state
            fold(rows, kt_ref[t], vt_ref[t], None)
            return t - 1, exhausted()

        lax.while_loop(cond, body, (own - 2, exhausted()))

    g = g_ref[...]
    o = jnp.where(_lo_lanes(), acc_ref[0], acc_ref[1])
    o_ref[...] = (o * (g * _sigmoid(g))).astype(o_ref.dtype)


def _attn_b_prompt(q, ktb, vtb, gate, tri):
    b, t, _ = q.shape
    bq = B_BQ
    assert bq % KV_TILE == 0 and t % bq == 0 and KV_TILE % TRI == 0
    q_spec = pl.BlockSpec((None, bq, LANES), lambda bi, h, qi: (bi, qi, h))
    kt_spec = pl.BlockSpec((None, t // KV_TILE, LANES, KV_TILE), lambda bi, h, qi: (bi, 0, h, 0))
    return pl.pallas_call(
        functools.partial(_attn_b_prompt_body, bq=bq, sub=KV_TILE),
        grid=(b, WIDTH // LANES, t // bq),
        in_specs=[pl.BlockSpec((TRI, TRI), lambda bi, h, qi: (0, 0)),
                  q_spec, kt_spec, kt_spec, q_spec],
        out_specs=q_spec,
        out_shape=jax.ShapeDtypeStruct((b, t, WIDTH), BF16),
        scratch_shapes=[pltpu.VMEM((2, bq, LANES), BF16),
                        pltpu.VMEM((2, bq, 1), F32),
                        pltpu.VMEM((2, bq, LANES), F32)],
        compiler_params=_params(3),
        name="attn_b_prompt",
    )(tri, q, ktb, vtb, gate)


def _pad_rows(x, rows):
    return jnp.concatenate([x, jnp.zeros((rows - x.shape[0], x.shape[1]), x.dtype)], axis=0)


def _attn_a_sample_body(lam_ref, subg_ref, q_ref, kn_ref, vn_ref, ktp_ref, vp_ref, g_ref, o_ref,
                        qq_ref, m_ref, l_ref, acc_ref, *, t_new, past, chunk, lam_init):
    pc = pl.program_id(1)

    @pl.when(pc == 0)
    def _():
        for h in range(A_HEADS):
            qq_ref[h] = jnp.concatenate(_halves(q_ref[:, h * LANES:(h + 1) * LANES]), axis=0)
        m_ref[...] = jnp.full(m_ref.shape, NEG_INF, F32)
        l_ref[...] = jnp.zeros(l_ref.shape, F32)
        acc_ref[...] = jnp.zeros(acc_ref.shape, F32)

    def update(h, s, v):
        m_prev = m_ref[h]
        m_new = jnp.maximum(m_prev, jnp.max(s, axis=-1, keepdims=True))
        alpha = jnp.exp2(m_prev - m_new)
        p = jnp.exp2(s - jnp.tile(m_new, (1, s.shape[1] // LANES)))
        l_ref[h] = alpha * l_ref[h] + jnp.sum(p, axis=-1, keepdims=True)
        acc_ref[h] = alpha * acc_ref[h] + jnp.dot(p.astype(BF16), v, preferred_element_type=F32)
        m_ref[h] = m_new

    for h in range(A_HEADS):
        kt = ktp_ref[h * LANES:(h + 1) * LANES, :].astype(BF16)
        v = vp_ref[pl.ds(h, chunk, stride=A_HEADS), :].astype(BF16)
        update(h, jnp.dot(qq_ref[h], kt, preferred_element_type=F32), v)

    @pl.when(pc == pl.num_programs(1) - 1)
    def _():
        row = lax.broadcasted_iota(jnp.int32, (2 * t_new, LANES), 0)
        col = lax.broadcasted_iota(jnp.int32, (2 * t_new, LANES), 1)
        vis = jnp.logical_and(col < t_new,
                              _chunk_of(past + col) <= _chunk_of(past + _mod_pow2(row, t_new)))
        for h in range(A_HEADS):
            cols = slice(h * LANES, (h + 1) * LANES)
            kn = _pad_rows(kn_ref[:, cols], LANES)
            vn = _pad_rows(vn_ref[:, cols], LANES)
            s_n = lax.dot_general(qq_ref[h], kn, _NT, preferred_element_type=F32)
            update(h, jnp.where(vis, s_n, NEG_INF), vn)
            o = acc_ref[h] / l_ref[h]
            _diff_epilogue(o[:t_new], o[t_new:], lam_ref, subg_ref, g_ref.at[:, cols], o_ref.at[:, cols],
                           lam_init)


def _attn_a_sample(q, kb, vb, kt_past, v_past, gate, lam4, subg, lam_init):
    b, t_new, _ = q.shape
    past = kt_past.shape[2]
    chunk = SAMPLE_CHUNK
    assert v_past.shape[1] == past * A_HEADS and past % chunk == 0
    new_spec = pl.BlockSpec((None, t_new, WIDTH), lambda bi, pc: (bi, 0, 0))
    ktp_spec = pl.BlockSpec((None, WIDTH, chunk), lambda bi, pc: (bi, 0, pc))
    vp_spec = pl.BlockSpec((None, chunk * A_HEADS, LANES), lambda bi, pc: (bi, pc, 0))
    full = lambda shape: pl.BlockSpec(shape, lambda bi, pc: (0,) * len(shape))
    state = pltpu.VMEM((A_HEADS, 2 * t_new, LANES), F32)
    return pl.pallas_call(
        functools.partial(_attn_a_sample_body, t_new=t_new, past=past, chunk=chunk, lam_init=lam_init),
        grid=(b, past // chunk),
        in_specs=[full(lam4.shape), full(subg.shape), new_spec, new_spec, new_spec,
                  ktp_spec, vp_spec, new_spec],
        out_specs=new_spec,
        out_shape=jax.ShapeDtypeStruct((b, t_new, WIDTH), BF16),
        scratch_shapes=[pltpu.VMEM((A_HEADS, 2 * t_new, LANES), BF16), state, state, state],
        compiler_params=_params(2),
        name="attn_a_sample",
    )(lam4, subg, q, kb, vb, kt_past, v_past, gate)


def _attn_b_sample_body(tri_ref, q_ref, kn_ref, vn_ref, g_ref, kt_hbm, vt_hbm, o_ref,
                        kbuf, vbuf, sems, qq_ref, carry_ref, acc_ref, *, t_new, past, chunk):
    bi = pl.program_id(0)
    rows = 2 * t_new
    n_blk = chunk // TRI
    n_chunks = past // chunk
    n_pairs = WIDTH // LANES
    slot = lax.rem(bi, 2)
    tri = tri_ref[...]

    def fetch(batch, ci, into):
        cols = pl.ds(pl.multiple_of(ci * chunk, chunk), chunk)
        return (pltpu.make_async_copy(kt_hbm.at[batch, :, cols], kbuf.at[into], sems.at[0, into]),
                pltpu.make_async_copy(vt_hbm.at[batch, :, cols], vbuf.at[into], sems.at[1, into]))

    @pl.when(bi == 0)
    def _():
        for cp in fetch(0, n_chunks - 1, 0):
            cp.start()

    @pl.when(bi + 1 < pl.num_programs(0))
    def _():
        for cp in fetch(bi + 1, n_chunks - 1, 1 - slot):
            cp.start()

    row = lax.broadcasted_iota(jnp.int32, (rows, TRI), 0)
    col = lax.broadcasted_iota(jnp.int32, (rows, TRI), 1)
    valid = col < _mod_pow2(row, t_new)
    for hp in range(n_pairs):
        cols = slice(hp * LANES, (hp + 1) * LANES)
        qq = jnp.concatenate(_halves(q_ref[:, cols]), axis=0)
        qq_ref[hp] = qq
        kn = _pad_rows(kn_ref[:, cols], TRI)
        vn = _pad_rows(vn_ref[:, cols], TRI)
        z_n = lax.dot_general(qq, kn, _NT, preferred_element_type=F32)
        a_n, carry = _stick_weights(z_n, valid, jnp.zeros((rows, 1), F32), tri)
        acc_ref[hp] = jnp.dot(a_n.astype(BF16), vn, preferred_element_type=F32)
        carry_ref[hp] = carry

    def absorb():
        for hp in range(n_pairs):
            feat = pl.ds(hp * LANES, LANES)
            z_all = jnp.dot(qq_ref[hp], kbuf[slot, feat, :].astype(BF16), preferred_element_type=F32)
            z = jnp.concatenate([z_all[:, c * TRI:(c + 1) * TRI] for c in range(n_blk)], axis=0)
            suffix = jnp.dot(_softplus2(z).astype(BF16), tri, preferred_element_type=F32)
            running = carry_ref[hp]
            carries = [None] * n_blk
            for c in range(n_blk - 1, -1, -1):
                carries[c] = running
                running = running + suffix[c * rows:(c + 1) * rows, :1]
            carry_ref[hp] = running
            a = jnp.exp2(jnp.minimum(z - suffix, 0.0) - jnp.concatenate(carries, axis=0)).astype(BF16)
            vtp = vbuf[slot, feat, :].astype(BF16)
            acc = acc_ref[hp]
            for c in range(n_blk):
                acc = acc + lax.dot_general(a[c * rows:(c + 1) * rows], vtp[:, c * TRI:(c + 1) * TRI], _NT,
                                            preferred_element_type=F32)
            acc_ref[hp] = acc

    def exhausted():
        return jnp.min(carry_ref[...]) >= STICK_EXIT

    for cp in fetch(bi, n_chunks - 1, slot):
        cp.wait()
    absorb()

    def cond(state):
        ci, done = state
        return jnp.logical_and(ci >= 0, jnp.logical_not(done))

    def body(state):
        ci, _ = state
        copies = fetch(bi, ci, slot)
        for cp in copies:
            cp.start()
        for cp in copies:
            cp.wait()
        absorb()
        return ci - 1, exhausted()

    lax.while_loop(cond, body, (jnp.int32(n_chunks - 2), exhausted()))

    for hp in range(n_pairs):
        cols = slice(hp * LANES, (hp + 1) * LANES)
        acc = acc_ref[hp]
        g = g_ref[:, cols]
        o = jnp.where(_lo_lanes(), acc[:t_new], acc[t_new:])
        o_ref[:, cols] = (o * (g * _sigmoid(g))).astype(o_ref.dtype)


def _attn_b_sample(q, kb, vb, kt_past, vt_past, gate, tri):
    b, t_new, _ = q.shape
    past = kt_past.shape[2]
    chunk = SAMPLE_B_CHUNK
    assert past % chunk == 0 and chunk % TRI == 0 and t_new <= TRI
    n_pairs = WIDTH // LANES
    new_spec = pl.BlockSpec((None, t_new, WIDTH), lambda bi: (bi, 0, 0))
    hbm_spec = pl.BlockSpec(memory_space=pl.ANY)
    return pl.pallas_call(
        functools.partial(_attn_b_sample_body, t_new=t_new, past=past, chunk=chunk),
        grid=(b,),
        in_specs=[pl.BlockSpec((TRI, TRI), lambda bi: (0, 0)),
                  new_spec, new_spec, new_spec, new_spec, hbm_spec, hbm_spec],
        out_specs=new_spec,
        out_shape=jax.ShapeDtypeStruct((b, t_new, WIDTH), BF16),
        scratch_shapes=[pltpu.VMEM((2, WIDTH, chunk), F32),
                        pltpu.VMEM((2, WIDTH, chunk), F32),
                        pltpu.SemaphoreType.DMA((2, 2)),
                        pltpu.VMEM((n_pairs, 2 * t_new, LANES), BF16),
                        pltpu.VMEM((n_pairs, 2 * t_new, 1), F32),
                        pltpu.VMEM((n_pairs, 2 * t_new, LANES), F32)],
        compiler_params=_params(1),
        name="attn_b_sample",
    )(tri, q, kb, vb, gate, kt_past, vt_past)


def _post_body(*refs, final):
    if final:
        o_ref, x_ref, p_ref, wo_ref, wg_ref, wp_ref, fg_ref, out_ref = refs
    else:
        o_ref, x_ref, p_ref, wo_ref, wg_ref, wp_ref, out_ref = refs
    x = x_ref[...] + jnp.dot(o_ref[...], wo_ref[...], preferred_element_type=F32)
    gate = _sigmoid(jnp.dot(x.astype(BF16), wg_ref[...], preferred_element_type=F32))
    x = x + gate * jnp.dot(p_ref[...].astype(BF16), wp_ref[...], preferred_element_type=F32)
    if final:
        ms = jnp.mean(x * x, axis=-1, keepdims=True)
        x = (x * lax.rsqrt(ms + EPS)) * fg_ref[...]
    out_ref[...] = x


def _post(o, x, p, layer, w_out, w_gate, w_proj, final_g):
    n = x.shape[0]
    rows = POST_ROWS
    assert n % rows == 0
    final = final_g is not None
    row_spec = lambda width: pl.BlockSpec((rows, width), lambda i: (i, 0))
    full = lambda shape: pl.BlockSpec(shape, lambda i: (0, 0))
    in_specs = [row_spec(WIDTH), row_spec(D_MODEL),
                pl.BlockSpec((None, rows, PLE_DIM), lambda i: (layer, i, 0)),
                full(w_out.shape), full(w_gate.shape), full(w_proj.shape)]
    args = [o, x, p, w_out, w_gate, w_proj]
    if final:
        in_specs.append(full((1, D_MODEL)))
        args.append(final_g.reshape(1, D_MODEL))
    return pl.pallas_call(
        functools.partial(_post_body, final=final),
        grid=(n // rows,),
        in_specs=in_specs,
        out_specs=row_spec(D_MODEL),
        out_shape=jax.ShapeDtypeStruct((n, D_MODEL), F32),
        compiler_params=_params(1),
        name="post_final" if final else "post",
    )(*args)


def _mixer_order(depth):
    return [("a", i // 2) if i % 2 == 0 else ("b", i // 2) for i in range(depth)]


def _lam_init(i):
    return 0.8 - 0.6 * math.exp(-0.3 * i)


def _trunk_prompt(x, p, pos, weights):
    (a_norm_g, a_w_in, a_lam, a_subln_g, a_w_out, b_norm_g, b_w_in, b_w_out,
     ple_w_proj, ple_w_gate, final_norm_g) = weights
    b, t, _ = x.shape
    n = b * t
    depth = p.shape[0]
    xf = x.reshape(n, D_MODEL)
    tables = _rope_tables(pos, PRE_ROWS)
    tri = jnp.tri(TRI, dtype=BF16)
    shp = (b, t, WIDTH)
    ak, av, bk, bv = [], [], [], []
    for i, (kind, j) in enumerate(_mixer_order(depth)):
        if kind == "a":
            q, kt, ktb, v, vb, gate = _pre_cols(xf, b, t, a_norm_g[j], a_w_in[j], tables, False)
            o = _attn_a_prompt(q.reshape(shp), ktb, vb.reshape(shp), gate.reshape(shp),
                               a_lam[j], a_subln_g[j].reshape(1, 2 * A_DH), _lam_init(i))
            ak.append(kt.reshape(b, A_HEADS, 2, A_DH, t).transpose(0, 4, 1, 2, 3))
            av.append(v.reshape(b, t, A_HEADS, 2 * A_DH))
            w_out = a_w_out[j]
        else:
            q, kt, ktb, vt, vtb, gate = _pre_cols(xf, b, t, b_norm_g[j], b_w_in[j], None, True)
            o = _attn_b_prompt(q.reshape(shp), ktb, vtb, gate.reshape(shp), tri)
            bk.append(kt.reshape(b, B_HEADS, B_DH, t).transpose(0, 3, 1, 2))
            bv.append(vt.reshape(b, B_HEADS, B_DH, t).transpose(0, 3, 1, 2))
            w_out = b_w_out[j]
        xf = _post(o.reshape(n, WIDTH), xf, p.reshape(depth, n, PLE_DIM), i, w_out,
                   ple_w_gate[i], ple_w_proj[i], final_norm_g if i == depth - 1 else None)
    return (xf.reshape(b, t, D_MODEL), jnp.stack(ak), jnp.stack(av), jnp.stack(bk), jnp.stack(bv))


def _trunk_sample(x, p, pos, weights, caches):
    (a_norm_g, a_w_in, a_lam, a_subln_g, a_w_out, b_norm_g, b_w_in, b_w_out,
     ple_w_proj, ple_w_gate, final_norm_g) = weights
    cache_a_k, cache_a_v, cache_b_k, cache_b_v = caches
    b, t, _ = x.shape
    n = b * t
    depth = p.shape[0]
    past = cache_a_k.shape[2]
    xf = x.reshape(n, D_MODEL)
    lane_tables, _ = _rope_tables(pos, PRE_ROWS)
    tri = jnp.tri(TRI, dtype=BF16)
    shp = (b, t, WIDTH)
    ak, av, bk, bv = [], [], [], []
    for i, (kind, j) in enumerate(_mixer_order(depth)):
        if kind == "a":
            q, k, kb, v, vb, gate = _pre_rows(xf, a_norm_g[j], a_w_in[j], lane_tables)
            kt_past = cache_a_k[j].transpose(0, 2, 3, 4, 1).reshape(b, WIDTH, past)
            v_past = cache_a_v[j].reshape(b, past * A_HEADS, 2 * A_DH)
            o = _attn_a_sample(q.reshape(shp), kb.reshape(shp), vb.reshape(shp), kt_past, v_past,
                               gate.reshape(shp), a_lam[j], a_subln_g[j].reshape(1, 2 * A_DH), _lam_init(i))
            ak.append(k.reshape(b, t, A_HEADS, 2, A_DH))
            av.append(v.reshape(b, t, A_HEADS, 2 * A_DH))
            w_out = a_w_out[j]
        else:
            q, k, kb, v, vb, gate = _pre_rows(xf, b_norm_g[j], b_w_in[j], None)
            kt_past = cache_b_k[j].transpose(0, 2, 3, 1).reshape(b, WIDTH, past)
            vt_past = cache_b_v[j].transpose(0, 2, 3, 1).reshape(b, WIDTH, past)
            o = _attn_b_sample(q.reshape(shp), kb.reshape(shp), vb.reshape(shp), kt_past, vt_past,
                               gate.reshape(shp), tri)
            bk.append(k.reshape(b, t, B_HEADS, B_DH))
            bv.append(v.reshape(b, t, B_HEADS, B_DH))
            w_out = b_w_out[j]
        xf = _post(o.reshape(n, WIDTH), xf, p.reshape(depth, n, PLE_DIM), i, w_out,
                   ple_w_gate[i], ple_w_proj[i], final_norm_g if i == depth - 1 else None)
    return (xf.reshape(b, t, D_MODEL), jnp.stack(ak), jnp.stack(av), jnp.stack(bk), jnp.stack(bv))


def kernel(x_prompt, x_sample, cache_a_k, cache_a_v, cache_b_k, cache_b_v, p_prompt, p_sample,
           a_norm_g, a_w_in, a_lam_q1, a_lam_k1, a_lam_q2, a_lam_k2, a_subln_g, a_w_out,
           b_norm_g, b_w_in, b_w_out, ple_w_proj, ple_w_gate, final_norm_g):
    past = cache_a_k.shape[2]
    a_lam = jnp.stack([a_lam_q1, a_lam_k1, a_lam_q2, a_lam_k2], axis=1)
    weights = (a_norm_g, a_w_in.astype(BF16), a_lam, a_subln_g, a_w_out.astype(BF16),
               b_norm_g, b_w_in.astype(BF16), b_w_out.astype(BF16),
               ple_w_proj.astype(BF16), ple_w_gate.astype(BF16), final_norm_g)
    pos_p = jnp.arange(x_prompt.shape[1], dtype=jnp.int32)
    pos_s = past + jnp.arange(x_sample.shape[1], dtype=jnp.int32)
    y_s, ak_s, av_s, bk_s, bv_s = _trunk_sample(x_sample, p_sample, pos_s, weights,
                                                (cache_a_k, cache_a_v, cache_b_k, cache_b_v))
    y_p, ak_p, av_p, bk_p, bv_p = _trunk_prompt(x_prompt, p_prompt, pos_p, weights)
    return (y_p, y_s, ak_p, av_p, bk_p, bv_p, ak_s, av_s, bk_s, bv_s)
```
